```python
import functools
import jax, jax.numpy as jnp
from jax import lax
import numpy as np

D_MODEL = 1024
BATCH = 4
SEQ = 4096
DEPTH = 2
DEC_BATCH = 128
DEC_SEQ = 8
PAST_LEN = 2048
PAGE_SIZE = 128

N_HEADS = 8
N_KV_HEADS = 4
HEAD_DIM = 64
ROT_DIM = HEAD_DIM // 4
ROPE_THETA = 500000.0
IDX_HEADS = 4
IDX_DIM = 64
IDX_ROT_DIM = IDX_DIM // 4
TOPK_MAX = 256
Q_BLOCK = 128
GLA_HEADS = 4
GLA_DK = 64
GLA_DV = 128
GLA_GATE_RANK = 16
GLA_TAU = 16.0
GLA_CHUNK = 64
D_FF = ((8 * D_MODEL + 3 * 256 - 1) // (3 * 256)) * 256
EPS = 1e-6

ATTN_W = N_HEADS * HEAD_DIM
KV_W = N_KV_HEADS * HEAD_DIM
GLA_K_W = GLA_HEADS * GLA_DK
GLA_V_W = GLA_HEADS * GLA_DV
IN_SPLITS = (ATTN_W, KV_W, KV_W, IDX_HEADS * IDX_DIM, IDX_DIM, IDX_HEADS,
             GLA_K_W, GLA_K_W, GLA_V_W, GLA_GATE_RANK, GLA_V_W, D_MODEL, D_MODEL)
N_IN = sum(IN_SPLITS)

kernel_name = 'hybrid_dsa_gla_decode_step'


def rms_norm(x, g):
    x32 = x.astype(jnp.float32)
    y = x32 * lax.rsqrt(jnp.mean(x32 * x32, axis=-1, keepdims=True) + EPS)
    return (y * g.astype(jnp.float32)).astype(x.dtype)


def partial_rope(x, pos, rot_dim):
    half = rot_dim // 2
    inv = jnp.power(ROPE_THETA, -jnp.arange(half, dtype=jnp.float32) * 2.0 / rot_dim)
    ang = pos.astype(jnp.float32)[:, None] * inv[None, :]
    cos = jnp.cos(ang)[None, :, None, :]
    sin = jnp.sin(ang)[None, :, None, :]
    xr = x[..., :rot_dim].astype(jnp.float32)
    x1, x2 = xr[..., :half], xr[..., half:]
    rot = jnp.concatenate([x1 * cos - x2 * sin, x2 * cos + x1 * sin], axis=-1).astype(x.dtype)
    return jnp.concatenate([rot, x[..., rot_dim:]], axis=-1)


def project_inputs(xn, pos, w_in, q_g, k_g, kidx_g, gla_wa2, gla_ba):
    B, T, _ = xn.shape
    h = xn @ w_in
    offs = np.cumsum(IN_SPLITS)[:-1].tolist()
    q, k, v, qi, ki, wi, gq, gk, gv, ga, gr, gate_a, gate_b = jnp.split(h, offs, axis=-1)
    q = partial_rope(rms_norm(q.reshape(B, T, N_HEADS, HEAD_DIM), q_g), pos, ROT_DIM)
    k = partial_rope(rms_norm(k.reshape(B, T, N_KV_HEADS, HEAD_DIM), k_g), pos, ROT_DIM)
    v = v.reshape(B, T, N_KV_HEADS, HEAD_DIM)
    qi = partial_rope(qi.reshape(B, T, IDX_HEADS, IDX_DIM), pos, IDX_ROT_DIM)
    ki = partial_rope(rms_norm(ki, kidx_g)[:, :, None, :], pos, IDX_ROT_DIM)[:, :, 0, :]
    wi = wi * (IDX_HEADS ** -0.5 * IDX_DIM ** -0.5)
    gq = gq.reshape(B, T, GLA_HEADS, GLA_DK) * GLA_DK ** -0.5
    gk = gk.reshape(B, T, GLA_HEADS, GLA_DK)
    gv = gv.reshape(B, T, GLA_HEADS, GLA_DV)
    log_a = jax.nn.log_sigmoid((ga @ gla_wa2 + gla_ba).astype(jnp.float32)) / GLA_TAU
    log_a = log_a.reshape(B, T, GLA_HEADS, GLA_DK)
    return q, k, v, qi, ki, wi, gq, gk, gv, log_a, gr, gate_a, gate_b


def gather_rows(k_all, v_all, idx):
    take = jax.vmap(lambda a, i: a[i])
    return take(k_all, idx), take(v_all, idx)


def gather_paged(cache_k, cache_v, layer, page_table, k_new, v_new, idx):
    ps = cache_k.shape[2]
    past_len = page_table.shape[1] * ps
    t_new = k_new.shape[1]
    is_past = (idx < past_len)[..., None, None]
    pidx = jnp.minimum(idx, past_len - 1)
    page = jax.vmap(lambda pt, i: pt[i])(page_table, pidx // ps)
    off = pidx % ps
    nidx = jnp.clip(idx - past_len, 0, t_new - 1)
    take = jax.vmap(lambda a, i: a[i])
    k_sel = jnp.where(is_past, cache_k[layer, page, off], take(k_new, nidx))
    v_sel = jnp.where(is_past, cache_v[layer, page, off], take(v_new, nidx))
    return k_sel, v_sel


def dsa_attention(q, qi, wi, q_pos, ki_all, topk, gather_kv):
    B, T = q.shape[:2]
    L = ki_all.shape[1]
    qb = min(Q_BLOCK, T)
    nb = -(-T // qb)
    pad = nb * qb - T

    def blocks(a):
        a = jnp.pad(a, [(0, 0), (0, pad)] + [(0, 0)] * (a.ndim - 2))
        return jnp.moveaxis(a.reshape((B, nb, qb) + a.shape[2:]), 1, 0)

    pos_b = jnp.pad(q_pos, (0, pad), mode='edge').reshape(nb, qb)
    k_pos = jnp.arange(L, dtype=jnp.int32)
    ki32 = ki_all.astype(jnp.float32)

    def one_block(args):
        qblk, qiblk, wiblk, pb = args
        s = jnp.einsum('bqhd,bsd->bqhs', qiblk.astype(jnp.float32), ki32)
        score = jnp.einsum('bqh,bqhs->bqs', wiblk.astype(jnp.float32), jax.nn.relu(s))
        causal = k_pos[None, :] <= pb[:, None]
        score = jnp.where(causal[None], score, -jnp.inf)
        _, idx = lax.top_k(score, topk)
        valid = idx <= pb[None, :, None]
        k_sel, v_sel = gather_kv(idx)
        qg = qblk.reshape(B, qb, N_KV_HEADS, N_HEADS // N_KV_HEADS, HEAD_DIM).astype(jnp.float32)
        logits = jnp.einsum('bqhgd,bqkhd->bqhgk', qg, k_sel.astype(jnp.float32)) * HEAD_DIM ** -0.5
        logits = jnp.where(valid[:, :, None, None, :], logits, -jnp.inf)
        p = jax.nn.softmax(logits, axis=-1)
        o = jnp.einsum('bqhgk,bqkhd->bqhgd', p, v_sel.astype(jnp.float32))
        return o.reshape(B, qb, ATTN_W).astype(q.dtype)

    out = lax.map(one_block, (blocks(q), blocks(qi), blocks(wi), pos_b))
    return jnp.moveaxis(out, 0, 1).reshape(B, nb * qb, ATTN_W)[:, :T]


def gla_recurrence(q, k, v, log_a, s0):
    B, T, H, _ = q.shape
    c = min(GLA_CHUNK, T)
    nc = -(-T // c)
    pad = nc * c - T

    def chunks(a):
        a = jnp.pad(a.astype(jnp.float32), [(0, 0), (0, pad), (0, 0), (0, 0)])
        return a.reshape(B, nc, c, H, a.shape[-1]).transpose(1, 0, 3, 2, 4)

    causal = jnp.tril(jnp.ones((c, c), dtype=bool))

    def step(S, inp):
        qc, kc, vc, lac = inp
        b = jnp.cumsum(lac, axis=2)
        o = jnp.einsum('bhtd,bhde->bhte', qc * jnp.exp(b), S)
        diff = b[:, :, :, None, :] - b[:, :, None, :, :]
        decay = jnp.exp(jnp.where(causal[:, :, None], diff, -jnp.inf))
        attn = jnp.einsum('bhtd,bhsd,bhtsd->bhts', qc, kc, decay)
        o = o + jnp.einsum('bhts,bhse->bhte', attn, vc)
        b_last = b[:, :, -1:, :]
        S = jnp.exp(b_last[:, :, 0, :])[..., None] * S + jnp.einsum(
            'bhsd,bhse->bhde', kc * jnp.exp(b_last - b), vc)
        return S, o

    S, o = lax.scan(step, s0.astype(jnp.float32), (chunks(q), chunks(k), chunks(v), chunks(log_a)))
    o = o.transpose(1, 0, 3, 2, 4).reshape(B, nc * c, H, GLA_DV)[:, :T]
    return o.astype(q.dtype), S.astype(s0.dtype)


def merge_and_ffn(x, attn_o, gla_o, gr, gate_a, gate_b, gla_on_g, w_oa, w_ob, w_o,
                  norm2_g, w_gu, w_down):
    B, T, _ = x.shape
    g_out = rms_norm(gla_o, gla_on_g).reshape(B, T, GLA_V_W) * jax.nn.silu(gr)
    merged = jax.nn.sigmoid(gate_a) * (attn_o @ w_oa) + jax.nn.sigmoid(gate_b) * (g_out @ w_ob)
    x = x + merged @ w_o
    hn = rms_norm(x, norm2_g)
    gt, up = jnp.split(hn @ w_gu, 2, axis=-1)
    return x + (jax.nn.silu(gt) * up) @ w_down


def setup_inputs(seed: int = 0) -> dict:
    key = jax.random.key(seed)
    ks = jax.random.split(key, 24)
    f32 = jnp.float32
    n_pages = PAST_LEN // PAGE_SIZE
    n_used = DEC_BATCH * n_pages
    n_pool = (5 * n_used + 3) // 4

    def nrm(k, shape, scale):
        return jax.random.normal(k, shape, f32) * scale

    page_table = jax.random.permutation(ks[0], n_pool)[:n_used].reshape(DEC_BATCH, n_pages).astype(jnp.int32)
    return {
        'x_prompt': nrm(ks[1], (BATCH, SEQ, D_MODEL), 1.0),
        'x_sample': nrm(ks[2], (DEC_BATCH, DEC_SEQ, D_MODEL), 1.0),
        'cache_k': nrm(ks[3], (DEPTH, n_pool, PAGE_SIZE, N_KV_HEADS, HEAD_DIM), 1.0),
        'cache_v': nrm(ks[4], (DEPTH, n_pool, PAGE_SIZE, N_KV_HEADS, HEAD_DIM), 1.0),
        'cache_kidx': nrm(ks[5], (DEPTH, n_pool, PAGE_SIZE, IDX_DIM), 1.0),
        'state_gla': nrm(ks[6], (DEPTH, DEC_BATCH, GLA_HEADS, GLA_DK, GLA_DV), 0.3),
        'page_table': page_table,
        'norm1_g': 1.0 + nrm(ks[7], (DEPTH, D_MODEL), 0.02),
        'w_in': nrm(ks[8], (DEPTH, D_MODEL, N_IN), D_MODEL ** -0.5),
        'q_norm_g': 1.0 + nrm(ks[9], (DEPTH, HEAD_DIM), 0.02),
        'k_norm_g': 1.0 + nrm(ks[10], (DEPTH, HEAD_DIM), 0.02),
        'kidx_norm_g': 1.0 + nrm(ks[11], (DEPTH, IDX_DIM), 0.02),
        'gla_wa2': nrm(ks[12], (DEPTH, GLA_GATE_RANK, GLA_K_W), GLA_GATE_RANK ** -0.5),
        'gla_ba': nrm(ks[13], (DEPTH, GLA_K_W), 0.02),
        'gla_onorm_g': 1.0 + nrm(ks[14], (DEPTH, GLA_DV), 0.02),
        'w_oa': nrm(ks[15], (DEPTH, ATTN_W, D_MODEL), ATTN_W ** -0.5),
        'w_ob': nrm(ks[16], (DEPTH, GLA_V_W, D_MODEL), GLA_V_W ** -0.5),
        'w_o': nrm(ks[17], (DEPTH, D_MODEL, D_MODEL), D_MODEL ** -0.5),
        'norm2_g': 1.0 + nrm(ks[18], (DEPTH, D_MODEL), 0.02),
        'w_ffn_gu': nrm(ks[19], (DEPTH, D_MODEL, 2 * D_FF), D_MODEL ** -0.5),
        'w_ffn_down': nrm(ks[20], (DEPTH, D_FF, D_MODEL), D_FF ** -0.5),
    }


def reference(x_prompt, x_sample, cache_k, cache_v, cache_kidx, state_gla, page_table,
              norm1_g, w_in, q_norm_g, k_norm_g, kidx_norm_g, gla_wa2, gla_ba, gla_onorm_g,
              w_oa, w_ob, w_o, norm2_g, w_ffn_gu, w_ffn_down):
    B, S, _ = x_prompt.shape
    Bd, Td, _ = x_sample.shape
    ps = cache_k.shape[2]
    past_len = page_table.shape[1] * ps
    pos_p = jnp.arange(S, dtype=jnp.int32)
    pos_s = past_len + jnp.arange(Td, dtype=jnp.int32)
    topk_p = min(TOPK_MAX, S // 4)
    topk_s = min(TOPK_MAX, (past_len + Td) // 4)

    hp, hs = x_prompt, x_sample
    nk_p, nv_p, nki_p, ns_p = [], [], [], []
    nk_s, nv_s, nki_s, ns_s = [], [], [], []
    for l in range(DEPTH):
        proj = functools.partial(project_inputs, w_in=w_in[l], q_g=q_norm_g[l], k_g=k_norm_g[l],
                                 kidx_g=kidx_norm_g[l], gla_wa2=gla_wa2[l], gla_ba=gla_ba[l])
        tail = functools.partial(merge_and_ffn, gla_on_g=gla_onorm_g[l], w_oa=w_oa[l], w_ob=w_ob[l],
                                 w_o=w_o[l], norm2_g=norm2_g[l], w_gu=w_ffn_gu[l], w_down=w_ffn_down[l])

        q, k, v, qi, ki, wi, gq, gk, gv, la, gr, ga, gb = proj(rms_norm(hp, norm1_g[l]), pos_p)
        attn_o = dsa_attention(q, qi, wi, pos_p, ki, topk_p, functools.partial(gather_rows, k, v))
        gla_o, s_fin = gla_recurrence(gq, gk, gv, la,
                                      jnp.zeros((B, GLA_HEADS, GLA_DK, GLA_DV), hp.dtype))
        hp = tail(hp, attn_o, gla_o, gr, ga, gb)
        nk_p.append(k); nv_p.append(v); nki_p.append(ki); ns_p.append(s_fin)

        q, k, v, qi, ki, wi, gq, gk, gv, la, gr, ga, gb = proj(rms_norm(hs, norm1_g[l]), pos_s)
        ki_all = jnp.concatenate(
            [cache_kidx[l, page_table].reshape(Bd, past_len, IDX_DIM).astype(ki.dtype), ki], axis=1)
        gather = functools.partial(gather_paged, cache_k, cache_v, l, page_table, k, v)
        attn_o = dsa_attention(q, qi, wi, pos_s, ki_all, topk_s, gather)
        gla_o, s_new = gla_recurrence(gq, gk, gv, la, state_gla[l])
        hs = tail(hs, attn_o, gla_o, gr, ga, gb)
        nk_s.append(k); nv_s.append(v); nki_s.append(ki); ns_s.append(s_new)

    y_prompt, y_sample = hp, hs
    new_k_prompt, new_v_prompt = jnp.stack(nk_p), jnp.stack(nv_p)
    new_kidx_prompt, new_gla_prompt = jnp.stack(nki_p), jnp.stack(ns_p)
    new_k_sample, new_v_sample = jnp.stack(nk_s), jnp.stack(nv_s)
    new_kidx_sample, new_gla_sample = jnp.stack(nki_s), jnp.stack(ns_s)
    return (y_prompt, y_sample, new_k_prompt, new_v_prompt, new_kidx_prompt, new_gla_prompt,
            new_k_sample, new_v_sample, new_kidx_sample, new_gla_sample)
```

```python
import functools

import numpy as np
import jax
import jax.numpy as jnp
from jax import lax
from jax.experimental import pallas as pl
from jax.experimental.pallas import tpu as pltpu

N_HEADS = 8
N_KV_HEADS = 4
HEAD_DIM = 64
ROT_DIM = HEAD_DIM // 4
ROPE_THETA = 500000.0
IDX_HEADS = 4
IDX_DIM = 64
IDX_ROT_DIM = IDX_DIM // 4
TOPK_MAX = 256
GLA_HEADS = 4
GLA_DK = 64
GLA_DV = 128
GLA_GATE_RANK = 16
GLA_TAU = 16.0
EPS = 1e-6

ATTN_W = N_HEADS * HEAD_DIM
KV_W = N_KV_HEADS * HEAD_DIM
IDXQ_W = IDX_HEADS * IDX_DIM
GLA_K_W = GLA_HEADS * GLA_DK
GLA_V_W = GLA_HEADS * GLA_DV

LANES = 128
VMEM_LIMIT = 56 * 1024 * 1024
GLA_SUB = 16
NEG_BIG = -1e30
KEY_NEG_INF = -2139095041

F32 = jnp.float32
BF16 = jnp.bfloat16
I32 = jnp.int32

_C_Q = 0
_C_K = _C_Q + ATTN_W
_C_V = _C_K + KV_W
_C_QI = _C_V + KV_W
_C_KIWI = _C_QI + IDXQ_W
_C_GQ = _C_KIWI + LANES
_C_GK = _C_GQ + GLA_K_W
_C_GV = _C_GK + GLA_K_W
_C_GA = _C_GV + GLA_V_W
_C_GR = _C_GA + LANES
NP_IN = _C_GR + GLA_V_W


def _cparams(sem):
    return pltpu.CompilerParams(dimension_semantics=sem, vmem_limit_bytes=VMEM_LIMIT)


def _split_hi_lo(x):
    hi = x.astype(BF16)
    lo = (x - hi.astype(F32)).astype(BF16)
    return hi, lo


def _dot(a, b):
    return jnp.dot(a, b, preferred_element_type=F32)


def _dot_nt(a, b):
    return lax.dot_general(a, b, (((1,), (1,)), ((), ())), preferred_element_type=F32)


def _dot_tn(a, b):
    return lax.dot_general(a, b, (((0,), (0,)), ((), ())), preferred_element_type=F32)


def _sigmoid(x):
    return 1.0 / (1.0 + jnp.exp(-x))


def _group_mean_sq(h, m_ref):
    hi, lo = _split_hi_lo(h * h)
    w = h.shape[-1]
    m = m_ref[:w, :w]
    return _dot(hi, m) + _dot(lo, m)


def _rope(h, cos, sa, sb):
    w = h.shape[-1]
    return h * cos + pltpu.roll(h, w - ROT_DIM // 2, 1) * sa + pltpu.roll(h, ROT_DIM // 2, 1) * sb


def _proj_kernel(x_ref, g1_ref, w_ref, qg_ref, kg_ref, kig_ref, wa2h_ref, wa2l_ref, ba_ref, tab_ref, m_ref,
                 q_o, k_o, v_o, kb_o, vb_o, qi_o, ki_o, kib_o, wi_o, gq_o, gk_o, gv_o, la_o, gr_o, ga_o, gb_o):
    d_model = x_ref.shape[-1]
    x = x_ref[...]
    ms = jnp.mean(x * x, axis=-1, keepdims=True)
    xn = ((x * lax.rsqrt(ms + EPS)) * g1_ref[...]).astype(BF16)

    def cols(a, n):
        return _dot(xn, w_ref[:, a:a + n])

    cos1, sa1, sb1 = tab_ref[:, :LANES], tab_ref[:, LANES:2 * LANES], tab_ref[:, 2 * LANES:]

    def tiled(t, w):
        return jnp.concatenate([t] * (w // LANES), axis=1) if w > LANES else t

    h = cols(_C_Q, ATTN_W)
    h = (h * lax.rsqrt(_group_mean_sq(h, m_ref) + EPS)) * qg_ref[...]
    h = _rope(h, tiled(cos1, ATTN_W), tiled(sa1, ATTN_W), tiled(sb1, ATTN_W))
    q_o[...] = (h * HEAD_DIM ** -0.5).astype(BF16)
    h = cols(_C_K, KV_W)
    h = (h * lax.rsqrt(_group_mean_sq(h, m_ref) + EPS)) * kg_ref[...]
    h = _rope(h, tiled(cos1, KV_W), tiled(sa1, KV_W), tiled(sb1, KV_W))
    k_o[...] = h
    kb_o[...] = h.astype(BF16)
    h = cols(_C_V, KV_W)
    v_o[...] = h
    vb_o[...] = h.astype(BF16)
    h = cols(_C_QI, IDXQ_W)
    h = _rope(h, tiled(cos1, IDXQ_W), tiled(sa1, IDXQ_W), tiled(sb1, IDXQ_W))
    qi_o[...] = h.astype(BF16)
    h = cols(_C_KIWI, LANES)
    wi_o[...] = h * (IDX_HEADS ** -0.5 * IDX_DIM ** -0.5)
    lane = lax.broadcasted_iota(I32, (1, LANES), 1)
    is_ki = lane < IDX_DIM
    hk = jnp.where(is_ki, h, 0.0)
    hk = (hk * lax.rsqrt(_group_mean_sq(hk, m_ref) + EPS)) * kig_ref[...]
    hk = _rope(hk, jnp.where(is_ki, cos1, 1.0), jnp.where(is_ki, sa1, 0.0), jnp.where(is_ki, sb1, 0.0))
    ki_o[...] = hk[:, :IDX_DIM]
    kib_o[...] = hk[:, :IDX_DIM].astype(BF16)
    gq_o[...] = cols(_C_GQ, GLA_K_W) * GLA_DK ** -0.5
    gk_o[...] = cols(_C_GK, GLA_K_W)
    gv_o[...] = cols(_C_GV, GLA_V_W)
    ga_hi, ga_lo = _split_hi_lo(cols(_C_GA, LANES))
    z = _dot(ga_hi, wa2h_ref[...]) + _dot(ga_lo, wa2h_ref[...]) + _dot(ga_hi, wa2l_ref[...]) + ba_ref[...]
    la_o[...] = (jnp.minimum(z, 0.0) - jnp.log1p(jnp.exp(-jnp.abs(z)))) / GLA_TAU
    gr_o[...] = cols(_C_GR, GLA_V_W)
    ga_o[...] = cols(NP_IN, d_model)
    gb_o[...] = cols(NP_IN + d_model, d_model)


def _proj(x, lw, tab, tab_blocks, tm):
    t, d_model = x.shape
    n_w = lw["w_in"].shape[1]
    grid = (t // tm,)
    row = lambda w: pl.BlockSpec((tm, w), lambda i: (i, 0))
    const = lambda a: pl.BlockSpec(a.shape, lambda i: (0,) * a.ndim)
    outs = [(ATTN_W, BF16), (KV_W, F32), (KV_W, F32), (KV_W, BF16), (KV_W, BF16), (IDXQ_W, BF16),
            (IDX_DIM, F32), (IDX_DIM, BF16), (LANES, F32), (GLA_K_W, F32), (GLA_K_W, F32), (GLA_V_W, F32),
            (GLA_K_W, F32), (GLA_V_W, F32), (d_model, F32), (d_model, F32)]
    consts = [lw["g1"], lw["w_in"], lw["qg"], lw["kg"], lw["kig"], lw["wa2h"], lw["wa2l"], lw["ba"]]
    return pl.pallas_call(
        _proj_kernel,
        grid=grid,
        in_specs=[row(d_model)] + [const(a) for a in consts]
        + [pl.BlockSpec((tm, 3 * LANES), lambda i: (i % tab_blocks, 0)), const(lw["mavg"])],
        out_specs=[row(w) for w, _ in outs],
        out_shape=[jax.ShapeDtypeStruct((t, w), dt) for w, dt in outs],
        compiler_params=_cparams(("parallel",)),
        name="proj",
    )(x, *consts, tab, lw["mavg"])


def _float_key(x):
    x = jnp.where(x == 0.0, 0.0, x)
    bits = pltpu.bitcast(x, I32)
    return bits ^ ((bits >> 31) & 0x7FFFFFFF)


def _count(mask):
    return jnp.sum(jnp.where(mask, 1.0, 0.0), axis=1, keepdims=True)


def _topk_select(get_keys, rows, n_keys, topk):
    kf = float(topk)
    cnt0 = _count(get_keys() >= 0)
    cur = jnp.where(cnt0 >= kf, 0, -2 ** 31).astype(I32)

    def bit_body(j, cur):
        cand = cur | jnp.left_shift(jnp.int32(1), 30 - j)
        return jnp.where(_count(get_keys() >= cand) >= kf, cand, cur)

    thr = lax.fori_loop(0, 31, bit_body, cur)
    keys = get_keys()
    need = kf - _count(keys > thr)
    excess = _count(keys == thr) - need
    excess = jnp.where(thr == KEY_NEG_INF, 0.0, excess)
    n_bits = max(1, int(np.ceil(np.log2(n_keys))))

    def find_cut():
        def body(j, v):
            cand = v | jnp.left_shift(jnp.int32(1), n_bits - 1 - j)
            pos = lax.broadcasted_iota(I32, (rows, n_keys), 1)
            c = _count((get_keys() == thr) & (pos < cand))
            return jnp.where(c < need, cand, v)
        return lax.fori_loop(0, n_bits, body, jnp.zeros((rows, 1), I32))

    cut = lax.cond(jnp.max(excess) > 0.0, find_cut, lambda: jnp.full((rows, 1), n_keys, I32))

    def mask_of(keys, pos):
        return (keys > thr) | ((keys == thr) & (pos <= cut))
    return mask_of


def _attn_prompt_kernel(q_ref, qi_ref, wi_ref, kb_ref, vb_ref, kib_ref, o_ref, key_ref, bias_ref, *, topk, kc):
    tq = q_ref.shape[0]
    n_keys = kb_ref.shape[0]
    nkc = n_keys // kc
    qpos = pl.program_id(1) * tq + lax.broadcasted_iota(I32, (tq, 1), 0)
    qi = qi_ref[...]
    wi = wi_ref[...]

    def score_chunk(c, carry):
        off = pl.multiple_of(c * kc, kc)
        kic = kib_ref[pl.ds(off, kc), :]
        acc = jnp.zeros((tq, kc), F32)
        for h in range(IDX_HEADS):
            s = _dot_nt(qi[:, h * IDX_DIM:(h + 1) * IDX_DIM], kic)
            acc = acc + wi[:, IDX_DIM + h:IDX_DIM + h + 1] * jnp.maximum(s, 0.0)
        kpos = off + lax.broadcasted_iota(I32, (1, kc), 1)
        acc = jnp.where(kpos <= qpos, acc, -jnp.inf)
        key_ref[:, pl.ds(off, kc)] = _float_key(acc)
        return carry

    lax.fori_loop(0, nkc, score_chunk, 0)
    mask_of = _topk_select(lambda: key_ref[...], tq, n_keys, topk)

    def bias_chunk(c, carry):
        off = pl.multiple_of(c * kc, kc)
        kpos = off + lax.broadcasted_iota(I32, (1, kc), 1)
        sel = mask_of(key_ref[:, pl.ds(off, kc)], kpos) & (kpos <= qpos)
        bias_ref[:, pl.ds(off, kc)] = jnp.where(sel, 0.0, NEG_BIG)
        return carry

    lax.fori_loop(0, nkc, bias_chunk, 0)

    for h in range(N_HEADS):
        g = h // (N_HEADS // N_KV_HEADS)
        hs = slice(h * HEAD_DIM, (h + 1) * HEAD_DIM)
        gs = slice(g * HEAD_DIM, (g + 1) * HEAD_DIM)
        qh = q_ref[:, hs]

        def chunk(c, carry):
            m, l, acc = carry
            off = pl.multiple_of(c * kc, kc)
            s = _dot_nt(qh, kb_ref[pl.ds(off, kc), gs]) + bias_ref[:, pl.ds(off, kc)]
            m_new = jnp.maximum(m, jnp.max(s, axis=1, keepdims=True))
            a = jnp.exp(m - m_new)
            p = jnp.exp(s - m_new)
            l = a * l + jnp.sum(p, axis=1, keepdims=True)
            acc = a * acc + _dot(p.astype(BF16), vb_ref[pl.ds(off, kc), gs])
            return m_new, l, acc

        init = (jnp.full((tq, 1), NEG_BIG, F32), jnp.zeros((tq, 1), F32), jnp.zeros((tq, HEAD_DIM), F32))
        _, l, acc = lax.fori_loop(0, nkc, chunk, init)
        o_ref[:, hs] = (acc / l).astype(o_ref.dtype)


def _attn_prompt(q, qi, wi, kb, vb, kib, n_seq, seq, topk, tq, kc):
    t = q.shape[0]
    nq = seq // tq
    qrow = lambda w: pl.BlockSpec((tq, w), lambda b, i: (b * nq + i, 0))
    krow = lambda w: pl.BlockSpec((seq, w), lambda b, i: (b, 0))
    return pl.pallas_call(
        functools.partial(_attn_prompt_kernel, topk=topk, kc=kc),
        grid=(n_seq, nq),
        in_specs=[qrow(ATTN_W), qrow(IDXQ_W), qrow(LANES), krow(KV_W), krow(KV_W), krow(IDX_DIM)],
        out_specs=qrow(ATTN_W),
        out_shape=jax.ShapeDtypeStruct((t, ATTN_W), BF16),
        scratch_shapes=[pltpu.VMEM((tq, seq), I32), pltpu.VMEM((tq, seq), F32)],
        compiler_params=_cparams(("parallel", "parallel")),
        name="attn_prompt",
    )(q, qi, wi, kb, vb, kib)


def _attn_sample_kernel(pt_ref, ck_ref, cv_ref, cki_ref, kn_ref, vn_ref, kin_ref, qi_ref, wi_ref, q_ref, o_ref,
                        ks_ref, vs_ref, kis_ref, key_ref, *, topk, past_len):
    p = pl.program_id(1)
    page = ck_ref.shape[0]
    t_new = kn_ref.shape[0]
    n_keys = ks_ref.shape[0]
    off = pl.multiple_of(p * page, page)
    ks_ref[pl.ds(off, page), :] = ck_ref[...].astype(BF16)
    vs_ref[pl.ds(off, page), :] = cv_ref[...].astype(BF16)
    kis_ref[pl.ds(off, page), :] = cki_ref[...].astype(BF16)

    @pl.when(p == pl.num_programs(1) - 1)
    def _():
        def tail(new):
            pad = jnp.zeros((n_keys - past_len - t_new, new.shape[1]), F32)
            return jnp.concatenate([new, pad], axis=0).astype(BF16)

        ks_ref[past_len:, :] = tail(kn_ref[...])
        vs_ref[past_len:, :] = tail(vn_ref[...])
        kis_ref[past_len:, :] = tail(kin_ref[...])

        s = wi_ref[...] * jnp.maximum(_dot_nt(qi_ref[...], kis_ref[...]), 0.0)
        score = s[0:t_new]
        for h in range(1, IDX_HEADS):
            score = score + s[h * t_new:(h + 1) * t_new]
        kpos = lax.broadcasted_iota(I32, (1, n_keys), 1)
        qpos = past_len + lax.broadcasted_iota(I32, (t_new, 1), 0)
        causal = kpos <= qpos
        key_ref[...] = _float_key(jnp.where(causal, score, -jnp.inf))
        mask_of = _topk_select(lambda: key_ref[...], t_new, n_keys, topk)
        bias = jnp.where(mask_of(key_ref[...], kpos) & causal, 0.0, NEG_BIG)
        rep = N_HEADS // N_KV_HEADS
        bias = jnp.concatenate([bias] * rep, axis=0)

        for g in range(N_KV_HEADS):
            gs = slice(g * HEAD_DIM, (g + 1) * HEAD_DIM)
            s = _dot_nt(q_ref[g], ks_ref[:, gs]) + bias
            m = jnp.max(s, axis=1, keepdims=True)
            pr = jnp.exp(s - m)
            l = jnp.sum(pr, axis=1, keepdims=True)
            o = _dot(pr.astype(BF16), vs_ref[:, gs]) / l
            for r in range(rep):
                h = g * rep + r
                o_ref[:, h * HEAD_DIM:(h + 1) * HEAD_DIM] = o[r * t_new:(r + 1) * t_new].astype(o_ref.dtype)


def _attn_sample(page_table, cache_k, cache_v, cache_kidx, layer, k_new, v_new, ki_new, qi_r, wi_r, q_r, topk):
    n_seq, n_pages = page_table.shape
    page = cache_k.shape[2]
    past_len = n_pages * page
    t_new = k_new.shape[0] // n_seq
    n_keys = past_len + LANES
    rep = N_HEADS // N_KV_HEADS
    pt = page_table.reshape(-1)
    cspec = lambda w: pl.BlockSpec((None, None, page, w), lambda b, p, pt: (layer, pt[b * n_pages + p], 0, 0))
    nspec = lambda w: pl.BlockSpec((t_new, w), lambda b, p, pt: (b, 0))
    return pl.pallas_call(
        functools.partial(_attn_sample_kernel, topk=topk, past_len=past_len),
        grid_spec=pltpu.PrefetchScalarGridSpec(
            num_scalar_prefetch=1,
            grid=(n_seq, n_pages),
            in_specs=[cspec(KV_W), cspec(KV_W), cspec(IDX_DIM), nspec(KV_W), nspec(KV_W), nspec(IDX_DIM),
                      pl.BlockSpec((None, IDX_HEADS * t_new, IDX_DIM), lambda b, p, pt: (b, 0, 0)),
                      pl.BlockSpec((None, IDX_HEADS * t_new, 1), lambda b, p, pt: (b, 0, 0)),
                      pl.BlockSpec((None, N_KV_HEADS, rep * t_new, HEAD_DIM), lambda b, p, pt: (b, 0, 0, 0))],
            out_specs=pl.BlockSpec((None, t_new, ATTN_W), lambda b, p, pt: (b, 0, 0)),
            scratch_shapes=[pltpu.VMEM((n_keys, KV_W), BF16), pltpu.VMEM((n_keys, KV_W), BF16),
                            pltpu.VMEM((n_keys, IDX_DIM), BF16), pltpu.VMEM((t_new, n_keys), I32)]),
        out_shape=jax.ShapeDtypeStruct((n_seq, t_new, ATTN_W), BF16),
        compiler_params=_cparams(("parallel", "arbitrary")),
        name="attn_sample",
    )(pt, cache_k, cache_v, cache_kidx, k_new, v_new, ki_new, qi_r, wi_r, q_r)


def _gla_kernel(q_ref, k_ref, v_ref, la_ref, s0_ref, o_ref, sf_ref, st_ref, *, sub):
    tb = q_ref.shape[0]

    @pl.when(pl.program_id(1) == 0)
    def _():
        st_ref[...] = s0_ref[...]

    r = lax.broadcasted_iota(I32, (sub, sub), 0)
    c = lax.broadcasted_iota(I32, (sub, sub), 1)
    causal = r >= c
    tril = jnp.where(causal, 1.0, 0.0).astype(BF16)

    def sub_chunk(i, carry):
        r0 = pl.multiple_of(i * sub, sub)
        rows = pl.ds(r0, sub)
        la = la_ref[rows, :]
        la_hi, la_mid = _split_hi_lo(la)
        la_lo = (la - la_hi.astype(F32) - la_mid.astype(F32)).astype(BF16)
        b = _dot(tril, la_hi) + _dot(tril, la_mid) + _dot(tril, la_lo)
        b_end = b[sub - 1:sub, :]
        b_mid = b[sub // 2 - 1:sub // 2, :]
        q = q_ref[rows, :]
        k = k_ref[rows, :]
        v = v_ref[rows, :].astype(BF16)
        q_in = (q * jnp.exp(b)).astype(BF16)
        q_d = (q * jnp.exp(b - b_mid)).astype(BF16)
        k_d = (k * jnp.exp(b_mid - b)).astype(BF16)
        k_out = (k * jnp.exp(b_end - b)).astype(BF16)
        dec = jnp.exp(b_end)
        for h in range(GLA_HEADS):
            ks = slice(h * GLA_DK, (h + 1) * GLA_DK)
            vs = slice(h * GLA_DV, (h + 1) * GLA_DV)
            st = st_ref[h]
            a = jnp.where(causal, _dot_nt(q_d[:, ks], k_d[:, ks]), 0.0).astype(BF16)
            o_ref[rows, vs] = _dot_nt(q_in[:, ks], st.astype(BF16)) + _dot(a, v[:, vs])
            st_ref[h] = st * dec[:, ks] + _dot_tn(v[:, vs], k_out[:, ks])
        return carry

    lax.fori_loop(0, tb // sub, sub_chunk, 0)
    sf_ref[...] = st_ref[...]


def _gla(gq, gk, gv, la, s0t, n_seq, seq, tb, sub):
    t = gq.shape[0]
    nt = seq // tb
    row = lambda w: pl.BlockSpec((tb, w), lambda b, i: (b * nt + i, 0))
    sspec = pl.BlockSpec((None, GLA_HEADS, GLA_DV, GLA_DK), lambda b, i: (b, 0, 0, 0))
    return pl.pallas_call(
        functools.partial(_gla_kernel, sub=sub),
        grid=(n_seq, nt),
        in_specs=[row(GLA_K_W), row(GLA_K_W), row(GLA_V_W), row(GLA_K_W), sspec],
        out_specs=[row(GLA_V_W), sspec],
        out_shape=[jax.ShapeDtypeStruct((t, GLA_V_W), F32),
                   jax.ShapeDtypeStruct((n_seq, GLA_HEADS, GLA_DV, GLA_DK), F32)],
        scratch_shapes=[pltpu.VMEM((GLA_HEADS, GLA_DV, GLA_DK), F32)],
        compiler_params=_cparams(("parallel", "arbitrary")),
        name="gla",
    )(gq, gk, gv, la, s0t)


def _merge_kernel(x_ref, at_ref, gl_ref, gr_ref, ga_ref, gb_ref, gon_ref, woa_ref, wob_ref, wo_ref, n2_ref,
                  x1_o, hn_o):
    gl = gl_ref[...]
    gr = gr_ref[...]
    parts = []
    for h in range(GLA_HEADS):
        gh = gl[:, h * GLA_DV:(h + 1) * GLA_DV]
        ms = jnp.mean(gh * gh, axis=-1, keepdims=True)
        parts.append((gh * lax.rsqrt(ms + EPS)) * gon_ref[...])
    g_out = jnp.concatenate(parts, axis=1) * (gr * _sigmoid(gr))
    merged = (_sigmoid(ga_ref[...]) * _dot(at_ref[...], woa_ref[...])
              + _sigmoid(gb_ref[...]) * _dot(g_out.astype(BF16), wob_ref[...]))
    x1 = x_ref[...] + _dot(merged.astype(BF16), wo_ref[...])
    x1_o[...] = x1
    ms = jnp.mean(x1 * x1, axis=-1, keepdims=True)
    hn_o[...] = ((x1 * lax.rsqrt(ms + EPS)) * n2_ref[...]).astype(BF16)


def _merge(x, attn_o, gla_o, gr, gate_a, gate_b, lw, tm):
    t, d_model = x.shape
    row = lambda w: pl.BlockSpec((tm, w), lambda i: (i, 0))
    const = lambda a: pl.BlockSpec(a.shape, lambda i: (0,) * a.ndim)
    consts = [lw["gon"], lw["w_oa"], lw["w_ob"], lw["w_o"], lw["n2"]]
    return pl.pallas_call(
        _merge_kernel,
        grid=(t // tm,),
        in_specs=[row(d_model), row(ATTN_W), row(GLA_V_W), row(GLA_V_W), row(d_model), row(d_model)]
        + [const(a) for a in consts],
        out_specs=[row(d_model), row(d_model)],
        out_shape=[jax.ShapeDtypeStruct((t, d_model), F32), jax.ShapeDtypeStruct((t, d_model), BF16)],
        compiler_params=_cparams(("parallel",)),
        name="merge",
    )(x, attn_o, gla_o, gr, gate_a, gate_b, *consts)


def _ffn_kernel(hn_ref, x1_ref, wg_ref, wu_ref, wd_ref, y_ref, acc_ref):
    f = pl.program_id(1)

    @pl.when(f == 0)
    def _():
        acc_ref[...] = jnp.zeros_like(acc_ref)

    hn = hn_ref[...]
    gt = _dot(hn, wg_ref[...])
    up = _dot(hn, wu_ref[...])
    acc_ref[...] += _dot(((gt * _sigmoid(gt)) * up).astype(BF16), wd_ref[...])

    @pl.when(f == pl.num_programs(1) - 1)
    def _():
        y_ref[...] = x1_ref[...] + acc_ref[...]


def _ffn(hn, x1, lw, tm, tf):
    t, d_model = x1.shape
    d_ff = lw["w_g"].shape[1]
    row = pl.BlockSpec((tm, d_model), lambda i, f: (i, 0))
    return pl.pallas_call(
        _ffn_kernel,
        grid=(t // tm, d_ff // tf),
        in_specs=[row, row,
                  pl.BlockSpec((d_model, tf), lambda i, f: (0, f)),
                  pl.BlockSpec((d_model, tf), lambda i, f: (0, f)),
                  pl.BlockSpec((tf, d_model), lambda i, f: (f, 0))],
        out_specs=row,
        out_shape=jax.ShapeDtypeStruct((t, d_model), F32),
        scratch_shapes=[pltpu.VMEM((tm, d_model), F32)],
        compiler_params=_cparams(("parallel", "arbitrary")),
        name="ffn",
    )(hn, x1, lw["w_g"], lw["w_u"], lw["w_d"])


def _pack_layer(l, norm1_g, w_in, q_norm_g, k_norm_g, kidx_norm_g, gla_wa2, gla_ba, gla_onorm_g,
                w_oa, w_ob, w_o, norm2_g, w_ffn_gu, w_ffn_down):
    d_model = w_in.shape[1]
    w = w_in[l]
    splits = (ATTN_W, KV_W, KV_W, IDXQ_W, IDX_DIM, IDX_HEADS, GLA_K_W, GLA_K_W, GLA_V_W, GLA_GATE_RANK,
              GLA_V_W, d_model, d_model)
    offs = np.cumsum(splits)[:-1].tolist()
    q, k, v, qi, ki, wi, gq, gk, gv, ga, gr, gate_a, gate_b = jnp.split(w, offs, axis=1)
    z = lambda n: jnp.zeros((d_model, n), w.dtype)
    packed = jnp.concatenate(
        [q, k, v, qi, ki, wi, z(LANES - IDX_DIM - IDX_HEADS), gq, gk, gv, ga, z(LANES - GLA_GATE_RANK),
         gr, gate_a, gate_b], axis=1).astype(BF16)
    wa2 = jnp.concatenate([gla_wa2[l], jnp.zeros((LANES - GLA_GATE_RANK, GLA_K_W), F32)], axis=0)
    wa2h, wa2l = _split_hi_lo(wa2)
    blk = np.arange(ATTN_W) // HEAD_DIM
    mavg = jnp.asarray((blk[:, None] == blk[None, :]).astype(np.float32) / HEAD_DIM, BF16)
    d_ff = w_ffn_down.shape[1]
    return dict(
        g1=norm1_g[l][None, :], w_in=packed,
        qg=jnp.tile(q_norm_g[l], N_HEADS)[None, :], kg=jnp.tile(k_norm_g[l], N_KV_HEADS)[None, :],
        kig=jnp.concatenate([kidx_norm_g[l], jnp.ones((LANES - IDX_DIM,), F32)])[None, :],
        wa2h=wa2h, wa2l=wa2l, ba=gla_ba[l][None, :], mavg=mavg,
        gon=gla_onorm_g[l][None, :], w_oa=w_oa[l].astype(BF16), w_ob=w_ob[l].astype(BF16),
        w_o=w_o[l].astype(BF16), n2=norm2_g[l][None, :],
        w_g=w_ffn_gu[l][:, :d_ff].astype(BF16), w_u=w_ffn_gu[l][:, d_ff:].astype(BF16),
        w_d=w_ffn_down[l].astype(BF16))


def _rope_table(pos):
    half = ROT_DIM // 2
    inv = jnp.power(ROPE_THETA, -jnp.arange(half, dtype=F32) * 2.0 / ROT_DIM)
    ang = pos.astype(F32)[:, None] * inv[None, :]
    cos, sin = jnp.cos(ang), jnp.sin(ang)
    n = pos.shape[0]
    rest = HEAD_DIM - ROT_DIM
    one = jnp.concatenate([cos, cos, jnp.ones((n, rest), F32)], axis=1)
    sa = jnp.concatenate([-sin, jnp.zeros((n, half + rest), F32)], axis=1)
    sb = jnp.concatenate([jnp.zeros((n, half), F32), sin, jnp.zeros((n, rest), F32)], axis=1)
    rep = LANES // HEAD_DIM
    return jnp.concatenate([jnp.tile(one, (1, rep)), jnp.tile(sa, (1, rep)), jnp.tile(sb, (1, rep))], axis=1)


def _tile_rows(n, want):
    t = min(n, want)
    while n % t:
        t //= 2
    return t


def _tail(x, attn_o, gla_o, gr, gate_a, gate_b, lw):
    t = x.shape[0]
    x1, hn = _merge(x, attn_o, gla_o, gr, gate_a, gate_b, lw, _tile_rows(t, 512))
    d_ff = lw["w_g"].shape[1]
    tf = d_ff // 2 if (d_ff // 2) % LANES == 0 else d_ff
    return _ffn(hn, x1, lw, _tile_rows(t, 512), tf)


def kernel(x_prompt, x_sample, cache_k, cache_v, cache_kidx, state_gla, page_table, norm1_g, w_in, q_norm_g,
           k_norm_g, kidx_norm_g, gla_wa2, gla_ba, gla_onorm_g, w_oa, w_ob, w_o, norm2_g, w_ffn_gu, w_ffn_down):
    n_b, seq, d_model = x_prompt.shape
    n_bd, t_d, _ = x_sample.shape
    depth = w_in.shape[0]
    n_pool, page = cache_k.shape[1], cache_k.shape[2]
    past_len = page_table.shape[1] * page
    topk_p = min(TOPK_MAX, seq // 4)
    topk_s = min(TOPK_MAX, (past_len + t_d) // 4)
    rep = N_HEADS // N_KV_HEADS

    tab_p = _rope_table(jnp.arange(seq, dtype=I32))
    tab_s = jnp.tile(_rope_table(past_len + jnp.arange(t_d, dtype=I32)), (n_bd, 1))
    ck = cache_k.reshape(depth, n_pool, page, KV_W)
    cv = cache_v.reshape(depth, n_pool, page, KV_W)

    hp = x_prompt.reshape(n_b * seq, d_model)
    hs = x_sample.reshape(n_bd * t_d, d_model)
    tm_p = _tile_rows(seq, 256)
    tm_s = _tile_rows(n_bd * t_d, 256)
    tq = _tile_rows(seq, 128)
    kc = _tile_rows(seq, 512)
    tb = _tile_rows(seq, 512)
    sub_p = _tile_rows(tb, GLA_SUB)
    outs = [[] for _ in range(8)]
    for l in range(depth):
        lw = _pack_layer(l, norm1_g, w_in, q_norm_g, k_norm_g, kidx_norm_g, gla_wa2, gla_ba, gla_onorm_g,
                         w_oa, w_ob, w_o, norm2_g, w_ffn_gu, w_ffn_down)

        (q, k, v, kb, vb, qi, ki, kib, wi, gq, gk, gv, la, gr, ga, gb) = _proj(hp, lw, tab_p, seq // tm_p, tm_p)
        attn_o = _attn_prompt(q, qi, wi, kb, vb, kib, n_b, seq, topk_p, tq, kc)
        s0 = jnp.zeros((n_b, GLA_HEADS, GLA_DV, GLA_DK), F32)
        gla_o, s_fin = _gla(gq, gk, gv, la, s0, n_b, seq, tb, sub_p)
        hp = _tail(hp, attn_o, gla_o, gr, ga, gb, lw)
        outs[0].append(k.reshape(n_b, seq, N_KV_HEADS, HEAD_DIM))
        outs[1].append(v.reshape(n_b, seq, N_KV_HEADS, HEAD_DIM))
        outs[2].append(ki.reshape(n_b, seq, IDX_DIM))
        outs[3].append(jnp.swapaxes(s_fin, 2, 3))

        (q, k, v, kb, vb, qi, ki, kib, wi, gq, gk, gv, la, gr, ga, gb) = _proj(
            hs, lw, tab_s, (n_bd * t_d) // tm_s, tm_s)
        qi_r = qi.reshape(n_bd, t_d, IDX_HEADS, IDX_DIM).transpose(0, 2, 1, 3).reshape(n_bd, IDX_HEADS * t_d, IDX_DIM)
        wi_r = wi[:, IDX_DIM:IDX_DIM + IDX_HEADS].reshape(n_bd, t_d, IDX_HEADS).transpose(0, 2, 1)
        wi_r = wi_r.reshape(n_bd, IDX_HEADS * t_d, 1)
        q_r = q.reshape(n_bd, t_d, N_KV_HEADS, rep, HEAD_DIM).transpose(0, 2, 3, 1, 4)
        q_r = q_r.reshape(n_bd, N_KV_HEADS, rep * t_d, HEAD_DIM)
        attn_o = _attn_sample(page_table, ck, cv, cache_kidx, l, k, v, ki, qi_r, wi_r, q_r, topk_s)
        attn_o = attn_o.reshape(n_bd * t_d, ATTN_W)
        gla_o, s_new = _gla(gq, gk, gv, la, jnp.swapaxes(state_gla[l], 2, 3), n_bd, t_d, t_d, t_d)
        hs = _tail(hs, attn_o, gla_o, gr, ga, gb, lw)
        outs[4].append(k.reshape(n_bd, t_d, N_KV_HEADS, HEAD_DIM))
        outs[5].append(v.reshape(n_bd, t_d, N_KV_HEADS, HEAD_DIM))
        outs[6].append(ki.reshape(n_bd, t_d, IDX_DIM))
        outs[7].append(jnp.swapaxes(s_new, 2, 3))

    y_prompt = hp.reshape(n_b, seq, d_model)
    y_sample = hs.reshape(n_bd, t_d, d_model)
    return (y_prompt, y_sample) + tuple(jnp.stack(o) for o in outs)
```

```python
import functools

import numpy as np
import jax
import jax.numpy as jnp
from jax import lax
from jax.experimental import pallas as pl
from jax.experimental.pallas import tpu as pltpu

N_HEADS = 8
N_KV_HEADS = 4
HEAD_DIM = 64
ROT_DIM = HEAD_DIM // 4
ROPE_THETA = 500000.0
IDX_HEADS = 4
IDX_DIM = 64
IDX_ROT_DIM = IDX_DIM // 4
TOPK_MAX = 256
GLA_HEADS = 4
GLA_DK = 64
GLA_DV = 128
GLA_GATE_RANK = 16
GLA_TAU = 16.0
EPS = 1e-6

ATTN_W = N_HEADS * HEAD_DIM
KV_W = N_KV_HEADS * HEAD_DIM
KV_REP = N_HEADS // N_KV_HEADS
IDXQ_W = IDX_HEADS * IDX_DIM
GLA_K_W = GLA_HEADS * GLA_DK
GLA_V_W = GLA_HEADS * GLA_DV

LANES = 128
SUBLANES = 8
VMEM_LIMIT = 56 * 1024 * 1024
GLA_SUB = 16
SAMPLE_GROUP = 8
NEG_BIG = -1e30
LOG2E = 1.4426950408889634
Q_SCALE = HEAD_DIM ** -0.5 * LOG2E
SHIFT_ROWS = 2
ONES_ROWS = 16
KEY_NEG_INF = -2139095041
INT_MIN = -2 ** 31

F32 = jnp.float32
BF16 = jnp.bfloat16
I32 = jnp.int32

_C_Q = 0
_C_K = _C_Q + ATTN_W
_C_V = _C_K + KV_W
_C_QI = _C_V + KV_W
_C_KIWI = _C_QI + IDXQ_W
_C_GQ = _C_KIWI + LANES
_C_GK = _C_GQ + GLA_K_W
_C_GV = _C_GK + GLA_K_W
_C_GA = _C_GV + GLA_V_W
_C_GR = _C_GA + LANES
NP_IN = _C_GR + GLA_V_W


def _cparams(sem):
    return pltpu.CompilerParams(dimension_semantics=sem, vmem_limit_bytes=VMEM_LIMIT)


def _split_hi_lo(x):
    hi = x.astype(BF16)
    lo = (x - hi.astype(F32)).astype(BF16)
    return hi, lo


def _split3(x):
    hi = x.astype(BF16)
    r = x - hi.astype(F32)
    mid = r.astype(BF16)
    return hi, mid, (r - mid.astype(F32)).astype(BF16)


def _dot(a, b):
    return jnp.dot(a, b, preferred_element_type=F32)


def _dot_nt(a, b):
    return lax.dot_general(a, b, (((1,), (1,)), ((), ())), preferred_element_type=F32)


def _dot_tn(a, b):
    return lax.dot_general(a, b, (((0,), (0,)), ((), ())), preferred_element_type=F32)


def _sigmoid(x):
    return 1.0 / (1.0 + jnp.exp(-x))


def _float_key(x):
    x = jnp.where(x == 0.0, 0.0, x)
    bits = pltpu.bitcast(x, I32)
    return bits ^ ((bits >> 31) & 0x7FFFFFFF)


def _group_mean_sq(h, m_ref):
    hi, lo = _split_hi_lo(h * h)
    w = h.shape[-1]
    m = m_ref[:w, :w]
    return _dot(hi, m) + _dot(lo, m)


def _rope(h, cos, sa, sb):
    w = h.shape[-1]
    return h * cos + pltpu.roll(h, w - ROT_DIM // 2, 1) * sa + pltpu.roll(h, ROT_DIM // 2, 1) * sb


def _proj_kernel(x_ref, g1_ref, w_ref, qg_ref, kg_ref, kig_ref, wa2h_ref, wa2l_ref, ba_ref, tab_ref, m_ref,
                 q_o, qt_o, k_o, v_o, kh_o, vt_o, qi_o, qit_o, ki_o, kib_o, wi_o, wit_o,
                 gq_o, gk_o, gv_o, la_o, gr_o, ga_o, gb_o):
    d_model = x_ref.shape[-1]
    x = x_ref[...]
    ms = jnp.mean(x * x, axis=-1, keepdims=True)
    xn = ((x * lax.rsqrt(ms + EPS)) * g1_ref[...]).astype(BF16)

    def cols(a, n):
        return _dot_nt(xn, w_ref[a:a + n, :])

    cos1, sa1, sb1 = tab_ref[:, :LANES], tab_ref[:, LANES:2 * LANES], tab_ref[:, 2 * LANES:]

    def tiled(t, w):
        return jnp.concatenate([t] * (w // LANES), axis=1) if w > LANES else t

    h = cols(_C_Q, ATTN_W)
    h = (h * lax.rsqrt(_group_mean_sq(h, m_ref) + EPS)) * qg_ref[...]
    h = _rope(h, tiled(cos1, ATTN_W), tiled(sa1, ATTN_W), tiled(sb1, ATTN_W)) * Q_SCALE
    q_o[...] = h.astype(BF16)
    qt_o[...] = h.T.astype(BF16)
    h = cols(_C_K, KV_W)
    h = (h * lax.rsqrt(_group_mean_sq(h, m_ref) + EPS)) * kg_ref[...]
    h = _rope(h, tiled(cos1, KV_W), tiled(sa1, KV_W), tiled(sb1, KV_W))
    k_o[...] = h
    lane64 = lax.broadcasted_iota(I32, (h.shape[0], HEAD_DIM), 1)
    shift_cols = jnp.where(lane64 < SHIFT_ROWS, 1.0, 0.0)
    for g in range(N_KV_HEADS):
        kh_o[g] = jnp.concatenate([h[:, g * HEAD_DIM:(g + 1) * HEAD_DIM], shift_cols], axis=1).astype(BF16)
    h = cols(_C_V, KV_W)
    v_o[...] = h
    ht = h.T
    for g in range(N_KV_HEADS):
        vt_o[g] = jnp.concatenate([ht[g * HEAD_DIM:(g + 1) * HEAD_DIM, :], jnp.ones((ONES_ROWS, ht.shape[1]), F32)],
                                  axis=0).astype(BF16)
    h = cols(_C_QI, IDXQ_W)
    h = _rope(h, tiled(cos1, IDXQ_W), tiled(sa1, IDXQ_W), tiled(sb1, IDXQ_W))
    qi_o[...] = h.astype(BF16)
    qit_o[...] = h.T.astype(BF16)
    h = cols(_C_KIWI, LANES)
    hw = h * (IDX_HEADS ** -0.5 * IDX_DIM ** -0.5)
    wi_o[...] = hw
    wit_o[...] = hw.T[IDX_DIM:IDX_DIM + SUBLANES, :]
    lane = lax.broadcasted_iota(I32, (1, LANES), 1)
    is_ki = lane < IDX_DIM
    hk = jnp.where(is_ki, h, 0.0)
    hk = (hk * lax.rsqrt(_group_mean_sq(hk, m_ref) + EPS)) * kig_ref[...]
    hk = _rope(hk, jnp.where(is_ki, cos1, 1.0), jnp.where(is_ki, sa1, 0.0), jnp.where(is_ki, sb1, 0.0))
    ki_o[...] = hk[:, :IDX_DIM]
    kib_o[...] = hk[:, :IDX_DIM].astype(BF16)
    gq_o[...] = cols(_C_GQ, GLA_K_W) * GLA_DK ** -0.5
    gk_o[...] = cols(_C_GK, GLA_K_W)
    gv_o[...] = cols(_C_GV, GLA_V_W)
    ga_hi, ga_lo = _split_hi_lo(cols(_C_GA, LANES))
    z = _dot(ga_hi, wa2h_ref[...]) + _dot(ga_lo, wa2h_ref[...]) + _dot(ga_hi, wa2l_ref[...]) + ba_ref[...]
    la_o[...] = (jnp.minimum(z, 0.0) - jnp.log1p(jnp.exp(-jnp.abs(z)))) / GLA_TAU
    gr_o[...] = cols(_C_GR, GLA_V_W)
    ga_o[...] = cols(NP_IN, d_model)
    gb_o[...] = cols(NP_IN + d_model, d_model)


_PROJ_NAMES = ("q", "qt", "k", "v", "kh", "vt", "qi", "qit", "ki", "kib", "wi", "wit",
               "gq", "gk", "gv", "la", "gr", "ga", "gb")


def _proj(x, lw, tab, tab_blocks, tm):
    t, d_model = x.shape
    row = lambda w: pl.BlockSpec((tm, w), lambda i: (i, 0))
    col = lambda w: pl.BlockSpec((w, tm), lambda i: (0, i))
    const = lambda a: pl.BlockSpec(a.shape, lambda i: (0,) * a.ndim)
    rsh = lambda w, dt: (row(w), jax.ShapeDtypeStruct((t, w), dt))
    csh = lambda w, dt: (col(w), jax.ShapeDtypeStruct((w, t), dt))
    outs = dict(
        q=rsh(ATTN_W, BF16), qt=csh(ATTN_W, BF16), k=rsh(KV_W, F32), v=rsh(KV_W, F32),
        kh=(pl.BlockSpec((N_KV_HEADS, tm, 2 * HEAD_DIM), lambda i: (0, i, 0)),
            jax.ShapeDtypeStruct((N_KV_HEADS, t, 2 * HEAD_DIM), BF16)),
        vt=(pl.BlockSpec((N_KV_HEADS, HEAD_DIM + ONES_ROWS, tm), lambda i: (0, 0, i)),
            jax.ShapeDtypeStruct((N_KV_HEADS, HEAD_DIM + ONES_ROWS, t), BF16)),
        qi=rsh(IDXQ_W, BF16), qit=csh(IDXQ_W, BF16), ki=rsh(IDX_DIM, F32),
        kib=rsh(IDX_DIM, BF16), wi=rsh(LANES, F32), wit=csh(SUBLANES, F32),
        gq=rsh(GLA_K_W, F32), gk=rsh(GLA_K_W, F32), gv=rsh(GLA_V_W, F32), la=rsh(GLA_K_W, F32),
        gr=rsh(GLA_V_W, F32), ga=rsh(d_model, F32), gb=rsh(d_model, F32))
    consts = [lw["g1"], lw["w_in_t"], lw["qg"], lw["kg"], lw["kig"], lw["wa2h"], lw["wa2l"], lw["ba"]]
    res = pl.pallas_call(
        _proj_kernel,
        grid=(t // tm,),
        in_specs=[row(d_model)] + [const(a) for a in consts]
        + [pl.BlockSpec((tm, 3 * LANES), lambda i: (i % tab_blocks, 0)), const(lw["mavg"])],
        out_specs=[outs[n][0] for n in _PROJ_NAMES],
        out_shape=[outs[n][1] for n in _PROJ_NAMES],
        compiler_params=_cparams(("parallel",)),
        name="proj",
    )(x, *consts, tab, lw["mavg"])
    return dict(zip(_PROJ_NAMES, res))


def _col_reduce8(x, op):
    r, c = x.shape
    blk = 8 * SUBLANES
    if r > blk and r % blk == 0:
        parts = x.reshape(r // blk, blk, c)
        x = parts[0]
        for j in range(1, r // blk):
            x = op(x, parts[j])
        r = blk
    while r > SUBLANES:
        r //= 2
        x = op(x[:r], x[r:])
    return x


def _topk_select_t(key_ref, nc, kc, tq, n_keys, topk):
    kf = float(topk)

    def count(pred):
        def body(c, acc):
            off = pl.multiple_of(c * kc, kc)
            kpos = off + lax.broadcasted_iota(I32, (kc, 1), 0)
            hit = pred(key_ref[pl.ds(off, kc), :], kpos)
            return acc + _col_reduce8(jnp.where(hit, 1.0, 0.0), jnp.add)
        acc = lax.fori_loop(0, nc, body, jnp.zeros((SUBLANES, tq), F32))
        return jnp.sum(acc, axis=0, keepdims=True)

    cur = jnp.where(count(lambda k, p: k >= 0) >= kf, 0, INT_MIN).astype(I32)

    def bit_body(j, cur):
        cand = cur | jnp.left_shift(jnp.int32(1), 30 - j)
        return jnp.where(count(lambda k, p: k >= cand) >= kf, cand, cur)

    thr = lax.fori_loop(0, 31, bit_body, cur)
    need = kf - count(lambda k, p: k > thr)
    excess = count(lambda k, p: k == thr) - need
    excess = jnp.where(thr <= KEY_NEG_INF, 0.0, excess)
    n_bits = max(1, int(np.ceil(np.log2(n_keys))))

    def find_cut():
        def body(j, v):
            cand = v | jnp.left_shift(jnp.int32(1), n_bits - 1 - j)
            c = count(lambda k, p: (k == thr) & (p < cand))
            return jnp.where(c < need, cand, v)
        return lax.fori_loop(0, n_bits, body, jnp.zeros((1, tq), I32))

    cut = lax.cond(jnp.max(excess) > 0.0, find_cut, lambda: jnp.full((1, tq), n_keys, I32))

    def mask_of(keys, kpos):
        return (keys > thr) | ((keys == thr) & (kpos <= cut))
    return mask_of


def _attn_prompt_kernel(qt_ref, qit_ref, wit_ref, kh_ref, vt_ref, kib_ref, o_ref,
                        key_ref, bias_ref, ot_ref, qa_ref, acc_ref,
                        *, topk, kc):
    tq = qt_ref.shape[1]
    n_keys = kib_ref.shape[0]
    i = pl.program_id(1)
    nc = ((i + 1) * tq + kc - 1) // kc
    qpos = i * tq + lax.broadcasted_iota(I32, (1, tq), 1)
    wit = wit_ref[...]

    def score_chunk(c, carry):
        off = pl.multiple_of(c * kc, kc)
        kic = kib_ref[pl.ds(off, kc), :]
        acc = jnp.zeros((kc, tq), F32)
        for h in range(IDX_HEADS):
            s = _dot(kic, qit_ref[h * IDX_DIM:(h + 1) * IDX_DIM, :])
            acc = acc + wit[h:h + 1, :] * jnp.maximum(s, 0.0)
        kpos = off + lax.broadcasted_iota(I32, (kc, 1), 0)
        key_ref[pl.ds(off, kc), :] = _float_key(jnp.where(kpos <= qpos, acc, -jnp.inf))
        return carry

    lax.fori_loop(0, nc, score_chunk, 0)
    mask_of = _topk_select_t(key_ref, nc, kc, tq, n_keys, topk)

    def bias_chunk(c, carry):
        off = pl.multiple_of(c * kc, kc)
        kpos = off + lax.broadcasted_iota(I32, (kc, 1), 0)
        sel = mask_of(key_ref[pl.ds(off, kc), :], kpos) & (kpos <= qpos)
        bias_ref[pl.ds(off, kc), :] = jnp.where(sel, 0.0, NEG_BIG)
        return carry

    lax.fori_loop(0, nc, bias_chunk, 0)

    row = lax.broadcasted_iota(I32, (2 * HEAD_DIM, 1), 0)

    def q_operand(g, m):
        heads = range(g * KV_REP, (g + 1) * KV_REP)
        qcat = jnp.concatenate([qt_ref[h * HEAD_DIM:(h + 1) * HEAD_DIM, :] for h in heads], axis=1).astype(F32)
        qa = jnp.concatenate([qcat, jnp.zeros_like(qcat)], axis=0)
        if m is not None:
            m_hi = m.astype(BF16).astype(F32)
            qa = jnp.where(row == HEAD_DIM, -m_hi, jnp.where(row == HEAD_DIM + 1, m_hi - m, qa))
        return qa.astype(BF16)

    def chunk_bias(c):
        off = pl.multiple_of(c * kc, kc)
        b = bias_ref[pl.ds(off, kc), :]
        return off, jnp.concatenate([b] * KV_REP, axis=1)

    def logits(g, off, b2):
        return _dot(kh_ref[g, pl.ds(off, kc), :], qa_ref[g]) + b2

    for g in range(N_KV_HEADS):
        qa_ref[g] = q_operand(g, None)

    def pass_max(c, mparts):
        off, b2 = chunk_bias(c)
        return tuple(jnp.maximum(mp, _col_reduce8(logits(g, off, b2), jnp.maximum)) for g, mp in enumerate(mparts))

    mparts = lax.fori_loop(0, nc, pass_max,
                           tuple(jnp.full((SUBLANES, KV_REP * tq), NEG_BIG, F32) for _ in range(N_KV_HEADS)))
    for g in range(N_KV_HEADS):
        qa_ref[g] = q_operand(g, jnp.max(mparts[g], axis=0, keepdims=True))
    acc_ref[...] = jnp.zeros_like(acc_ref)

    def pass_acc(c, carry):
        off, b2 = chunk_bias(c)
        for g in range(N_KV_HEADS):
            pb = jnp.exp2(logits(g, off, b2)).astype(BF16)
            vt = vt_ref[g, :, pl.ds(off, kc)]
            for r in range(KV_REP):
                acc_ref[g * KV_REP + r] += _dot(vt, pb[:, r * tq:(r + 1) * tq])
        return carry

    lax.fori_loop(0, nc, pass_acc, 0)
    for h in range(N_HEADS):
        ot_ref[h * HEAD_DIM:(h + 1) * HEAD_DIM, :] = acc_ref[h, :HEAD_DIM, :] / acc_ref[h, HEAD_DIM:HEAD_DIM + 1, :]

    o_ref[...] = ot_ref[...].T.astype(o_ref.dtype)


def _attn_prompt(p, n_seq, seq, topk, tq, kc):
    t = p["qt"].shape[1]
    nq = seq // tq
    qcol = lambda w: pl.BlockSpec((w, tq), lambda b, i: (0, b * nq + i))
    return pl.pallas_call(
        functools.partial(_attn_prompt_kernel, topk=topk, kc=kc),
        grid=(n_seq, nq),
        in_specs=[qcol(ATTN_W), qcol(IDXQ_W), qcol(SUBLANES),
                  pl.BlockSpec((N_KV_HEADS, seq, 2 * HEAD_DIM), lambda b, i: (0, b, 0)),
                  pl.BlockSpec((N_KV_HEADS, HEAD_DIM + ONES_ROWS, seq), lambda b, i: (0, 0, b)),
                  pl.BlockSpec((seq, IDX_DIM), lambda b, i: (b, 0))],
        out_specs=pl.BlockSpec((tq, ATTN_W), lambda b, i: (b * nq + i, 0)),
        out_shape=jax.ShapeDtypeStruct((t, ATTN_W), BF16),
        scratch_shapes=[pltpu.VMEM((seq, tq), I32), pltpu.VMEM((seq, tq), F32), pltpu.VMEM((ATTN_W, tq), F32),
                        pltpu.VMEM((N_KV_HEADS, 2 * HEAD_DIM, KV_REP * tq), BF16),
                        pltpu.VMEM((N_HEADS, HEAD_DIM + ONES_ROWS, tq), F32)],
        compiler_params=_cparams(("parallel", "parallel")),
        name="attn_prompt",
    )(p["qt"], p["qit"], p["wit"], p["kh"], p["vt"], p["kib"])


def _count_rows(mask):
    return jnp.sum(jnp.where(mask, 1.0, 0.0), axis=1, keepdims=True)


def _topk_select(key_ref, rows, n_keys, topk):
    kf = float(topk)
    cur = jnp.where(_count_rows(key_ref[...] >= 0) >= kf, 0, INT_MIN).astype(I32)

    def bit_body(j, cur):
        cand = cur | jnp.left_shift(jnp.int32(1), 30 - j)
        return jnp.where(_count_rows(key_ref[...] >= cand) >= kf, cand, cur)

    thr = lax.fori_loop(0, 31, bit_body, cur)
    need = kf - _count_rows(key_ref[...] > thr)
    excess = _count_rows(key_ref[...] == thr) - need
    excess = jnp.where(thr <= KEY_NEG_INF, 0.0, excess)
    n_bits = max(1, int(np.ceil(np.log2(n_keys))))

    def find_cut():
        def body(j, v):
            cand = v | jnp.left_shift(jnp.int32(1), n_bits - 1 - j)
            pos = lax.broadcasted_iota(I32, (rows, n_keys), 1)
            c = _count_rows((key_ref[...] == thr) & (pos < cand))
            return jnp.where(c < need, cand, v)
        return lax.fori_loop(0, n_bits, body, jnp.zeros((rows, 1), I32))

    cut = lax.cond(jnp.max(excess) > 0.0, find_cut, lambda: jnp.full((rows, 1), n_keys, I32))

    def mask_of(keys, pos):
        return (keys > thr) | ((keys == thr) & (pos <= cut))
    return mask_of


def _attn_sample_kernel(pt_ref, *refs, topk, past_len, group):
    ck, cv, cki = refs[:group], refs[group:2 * group], refs[2 * group:3 * group]
    (kn_ref, vn_ref, kin_ref, qi_ref, wi_ref, q_ref, o_ref,
     ks_ref, vs_ref, kis_ref, key_ref, bias_ref) = refs[3 * group:]
    p = pl.program_id(1)
    page = ck[0].shape[1]
    t_new = kn_ref.shape[1]
    n_keys = past_len + LANES
    rows = group * t_new
    off = pl.multiple_of(p * page, page)
    for j in range(group):
        ks_ref[j, :, pl.ds(off, page)] = ck[j][...].astype(BF16)
        vs_ref[j, :, pl.ds(off, page)] = cv[j][...].astype(BF16)
        kis_ref[j, :, pl.ds(off, page)] = cki[j][...].astype(BF16)

    @pl.when(p == pl.num_programs(1) - 1)
    def _():
        def padded(new):
            pad = jnp.zeros((LANES - t_new, new.shape[1]), F32)
            return jnp.concatenate([new, pad], axis=0).astype(BF16)

        kpos = lax.broadcasted_iota(I32, (1, n_keys), 1)
        qpos1 = past_len + lax.broadcasted_iota(I32, (t_new, 1), 0)
        for j in range(group):
            qi = qi_ref[j]
            s = jnp.concatenate([_dot(qi, kis_ref[j]), _dot_nt(qi, padded(kin_ref[j]))], axis=1)
            s = wi_ref[j] * jnp.maximum(s, 0.0)
            score = s[0:t_new]
            for h in range(1, IDX_HEADS):
                score = score + s[h * t_new:(h + 1) * t_new]
            key_ref[j * t_new:(j + 1) * t_new, :] = _float_key(jnp.where(kpos <= qpos1, score, -jnp.inf))

        mask_of = _topk_select(key_ref, rows, n_keys, topk)
        qpos = past_len + lax.rem(lax.broadcasted_iota(I32, (rows, 1), 0), t_new)
        bias_ref[...] = jnp.where(mask_of(key_ref[...], kpos) & (kpos <= qpos), 0.0, NEG_BIG)

        for j in range(group):
            b1 = bias_ref[j * t_new:(j + 1) * t_new, :]
            bias = jnp.concatenate([b1] * KV_REP, axis=0)
            kn = padded(kn_ref[j])
            vn = padded(vn_ref[j])
            for g in range(N_KV_HEADS):
                gs = slice(g * HEAD_DIM, (g + 1) * HEAD_DIM)
                qg = q_ref[j, g]
                s = jnp.concatenate([_dot(qg, ks_ref[j, gs, :]), _dot_nt(qg, kn[:, gs])], axis=1) + bias
                m = jnp.max(s, axis=1, keepdims=True)
                pr = jnp.exp2(s - m)
                l = jnp.sum(pr, axis=1, keepdims=True)
                pb = pr.astype(BF16)
                o = (_dot_nt(pb[:, :past_len], vs_ref[j, gs, :]) + _dot(pb[:, past_len:], vn[:, gs])) / l
                for r in range(KV_REP):
                    h = g * KV_REP + r
                    o_ref[j, :, h * HEAD_DIM:(h + 1) * HEAD_DIM] = o[r * t_new:(r + 1) * t_new].astype(o_ref.dtype)


def _attn_sample(page_table, ckt, cvt, ckit, layer, k_new, v_new, ki_new, qi_r, wi_r, q_r, topk, group):
    n_seq, n_pages = page_table.shape
    page = ckt.shape[3]
    past_len = n_pages * page
    t_new = k_new.shape[1]
    n_keys = past_len + LANES
    rows = group * t_new
    pt = page_table.reshape(-1)

    def cspec(w, j):
        return pl.BlockSpec((None, None, w, page),
                            lambda b, p, pt: (layer, pt[(b * group + j) * n_pages + p], 0, 0))

    gspec = lambda *s: pl.BlockSpec((group,) + s, lambda b, p, pt: (b,) + (0,) * len(s))
    return pl.pallas_call(
        functools.partial(_attn_sample_kernel, topk=topk, past_len=past_len, group=group),
        grid_spec=pltpu.PrefetchScalarGridSpec(
            num_scalar_prefetch=1,
            grid=(n_seq // group, n_pages),
            in_specs=[cspec(KV_W, j) for j in range(group)] + [cspec(KV_W, j) for j in range(group)]
            + [cspec(IDX_DIM, j) for j in range(group)]
            + [gspec(t_new, KV_W), gspec(t_new, KV_W), gspec(t_new, IDX_DIM),
               gspec(IDX_HEADS * t_new, IDX_DIM), gspec(IDX_HEADS * t_new, 1),
               gspec(N_KV_HEADS, KV_REP * t_new, HEAD_DIM)],
            out_specs=gspec(t_new, ATTN_W),
            scratch_shapes=[pltpu.VMEM((group, KV_W, past_len), BF16), pltpu.VMEM((group, KV_W, past_len), BF16),
                            pltpu.VMEM((group, IDX_DIM, past_len), BF16),
                            pltpu.VMEM((rows, n_keys), I32), pltpu.VMEM((rows, n_keys), F32)]),
        out_shape=jax.ShapeDtypeStruct((n_seq, t_new, ATTN_W), BF16),
        compiler_params=_cparams(("parallel", "arbitrary")),
        name="attn_sample",
    )(pt, *([ckt] * group), *([cvt] * group), *([ckit] * group), k_new, v_new, ki_new, qi_r, wi_r, q_r)


def _gla_kernel(q_ref, k_ref, v_ref, la_ref, s0_ref, o_ref, sf_ref, st_ref, *, sub):
    tb = q_ref.shape[0]

    @pl.when(pl.program_id(1) == 0)
    def _():
        st_ref[...] = s0_ref[...]

    r = lax.broadcasted_iota(I32, (sub, sub), 0)
    c = lax.broadcasted_iota(I32, (sub, sub), 1)
    causal = r >= c
    tril = jnp.where(causal, 1.0, 0.0).astype(BF16)
    ones = jnp.ones((sub, GLA_DV), BF16)

    def sub_chunk(i, carry):
        r0 = pl.multiple_of(i * sub, sub)
        rows = pl.ds(r0, sub)
        la3 = _split3(la_ref[rows, :])
        b = sum(_dot(tril, part) for part in la3)
        b_end = b[sub - 1:sub, :]
        b_mid = b[sub // 2 - 1:sub // 2, :]
        q = q_ref[rows, :]
        k = k_ref[rows, :]
        v = v_ref[rows, :].astype(BF16)
        q_in = (q * jnp.exp(b)).astype(BF16)
        q_d = (q * jnp.exp(b - b_mid)).astype(BF16)
        k_d = (k * jnp.exp(b_mid - b)).astype(BF16)
        k_out = (k * jnp.exp(b_end - b)).astype(BF16)
        for h in range(GLA_HEADS):
            ks = slice(h * GLA_DK, (h + 1) * GLA_DK)
            vs = slice(h * GLA_DV, (h + 1) * GLA_DV)
            tot = sum(_dot_tn(part[:, ks], ones) for part in la3)
            st = st_ref[h]
            a = jnp.where(causal, _dot_nt(q_d[:, ks], k_d[:, ks]), 0.0).astype(BF16)
            o_ref[rows, vs] = _dot(q_in[:, ks], st.astype(BF16)) + _dot(a, v[:, vs])
            st_ref[h] = st * jnp.exp(tot) + _dot_tn(k_out[:, ks], v[:, vs])
        return carry

    lax.fori_loop(0, tb // sub, sub_chunk, 0)
    sf_ref[...] = st_ref[...]


def _gla(gq, gk, gv, la, s0, n_seq, seq, tb, sub):
    t = gq.shape[0]
    nt = seq // tb
    row = lambda w: pl.BlockSpec((tb, w), lambda b, i: (b * nt + i, 0))
    sspec = pl.BlockSpec((None, GLA_HEADS, GLA_DK, GLA_DV), lambda b, i: (b, 0, 0, 0))
    return pl.pallas_call(
        functools.partial(_gla_kernel, sub=sub),
        grid=(n_seq, nt),
        in_specs=[row(GLA_K_W), row(GLA_K_W), row(GLA_V_W), row(GLA_K_W), sspec],
        out_specs=[row(GLA_V_W), sspec],
        out_shape=[jax.ShapeDtypeStruct((t, GLA_V_W), F32),
                   jax.ShapeDtypeStruct((n_seq, GLA_HEADS, GLA_DK, GLA_DV), F32)],
        scratch_shapes=[pltpu.VMEM((GLA_HEADS, GLA_DK, GLA_DV), F32)],
        compiler_params=_cparams(("parallel", "arbitrary")),
        name="gla",
    )(gq, gk, gv, la, s0)


def _merge_kernel(x_ref, at_ref, gl_ref, gr_ref, ga_ref, gb_ref, gon_ref, woa_ref, wob_ref, wo_ref, n2_ref,
                  x1_o, hn_o):
    gl = gl_ref[...]
    gr = gr_ref[...]
    parts = []
    for h in range(GLA_HEADS):
        gh = gl[:, h * GLA_DV:(h + 1) * GLA_DV]
        ms = jnp.mean(gh * gh, axis=-1, keepdims=True)
        parts.append((gh * lax.rsqrt(ms + EPS)) * gon_ref[...])
    g_out = jnp.concatenate(parts, axis=1) * (gr * _sigmoid(gr))
    merged = (_sigmoid(ga_ref[...]) * _dot(at_ref[...], woa_ref[...])
              + _sigmoid(gb_ref[...]) * _dot(g_out.astype(BF16), wob_ref[...]))
    x1 = x_ref[...] + _dot(merged.astype(BF16), wo_ref[...])
    x1_o[...] = x1
    ms = jnp.mean(x1 * x1, axis=-1, keepdims=True)
    hn_o[...] = ((x1 * lax.rsqrt(ms + EPS)) * n2_ref[...]).astype(BF16)


def _merge(x, attn_o, gla_o, gr, gate_a, gate_b, lw, tm):
    t, d_model = x.shape
    row = lambda w: pl.BlockSpec((tm, w), lambda i: (i, 0))
    const = lambda a: pl.BlockSpec(a.shape, lambda i: (0,) * a.ndim)
    consts = [lw["gon"], lw["w_oa"], lw["w_ob"], lw["w_o"], lw["n2"]]
    return pl.pallas_call(
        _merge_kernel,
        grid=(t // tm,),
        in_specs=[row(d_model), row(ATTN_W), row(GLA_V_W), row(GLA_V_W), row(d_model), row(d_model)]
        + [const(a) for a in consts],
        out_specs=[row(d_model), row(d_model)],
        out_shape=[jax.ShapeDtypeStruct((t, d_model), F32), jax.ShapeDtypeStruct((t, d_model), BF16)],
        compiler_params=_cparams(("parallel",)),
        name="merge",
    )(x, attn_o, gla_o, gr, gate_a, gate_b, *consts)


def _ffn_kernel(hn_ref, x1_ref, wg_ref, wu_ref, wd_ref, y_ref, acc_ref):
    f = pl.program_id(1)

    @pl.when(f == 0)
    def _():
        acc_ref[...] = jnp.zeros_like(acc_ref)

    hn = hn_ref[...]
    gt = _dot(hn, wg_ref[...])
    up = _dot(hn, wu_ref[...])
    acc_ref[...] += _dot(((gt * _sigmoid(gt)) * up).astype(BF16), wd_ref[...])

    @pl.when(f == pl.num_programs(1) - 1)
    def _():
        y_ref[...] = x1_ref[...] + acc_ref[...]


def _ffn(hn, x1, lw, tm, tf):
    t, d_model = x1.shape
    d_ff = lw["w_g"].shape[1]
    row = pl.BlockSpec((tm, d_model), lambda i, f: (i, 0))
    return pl.pallas_call(
        _ffn_kernel,
        grid=(t // tm, d_ff // tf),
        in_specs=[row, row,
                  pl.BlockSpec((d_model, tf), lambda i, f: (0, f)),
                  pl.BlockSpec((d_model, tf), lambda i, f: (0, f)),
                  pl.BlockSpec((tf, d_model), lambda i, f: (f, 0))],
        out_specs=row,
        out_shape=jax.ShapeDtypeStruct((t, d_model), F32),
        scratch_shapes=[pltpu.VMEM((tm, d_model), F32)],
        compiler_params=_cparams(("parallel", "arbitrary")),
        name="ffn",
    )(hn, x1, lw["w_g"], lw["w_u"], lw["w_d"])


def _pack_layer(l, norm1_g, w_in, q_norm_g, k_norm_g, kidx_norm_g, gla_wa2, gla_ba, gla_onorm_g,
                w_oa, w_ob, w_o, norm2_g, w_ffn_gu, w_ffn_down):
    d_model = w_in.shape[1]
    wt = jnp.transpose(w_in, (2, 0, 1))[:, l, :]
    splits = (ATTN_W, KV_W, KV_W, IDXQ_W, IDX_DIM, IDX_HEADS, GLA_K_W, GLA_K_W, GLA_V_W, GLA_GATE_RANK,
              GLA_V_W, d_model, d_model)
    offs = np.cumsum(splits)[:-1].tolist()
    q, k, v, qi, ki, wi, gq, gk, gv, ga, gr, gate_a, gate_b = jnp.split(wt, offs, axis=0)
    z = lambda n: jnp.zeros((n, d_model), wt.dtype)
    packed = jnp.concatenate(
        [q, k, v, qi, ki, wi, z(LANES - IDX_DIM - IDX_HEADS), gq, gk, gv, ga, z(LANES - GLA_GATE_RANK),
         gr, gate_a, gate_b], axis=0).astype(BF16)
    wa2 = jnp.concatenate([gla_wa2[l], jnp.zeros((LANES - GLA_GATE_RANK, GLA_K_W), F32)], axis=0)
    wa2h, wa2l = _split_hi_lo(wa2)
    blk = np.arange(ATTN_W) // HEAD_DIM
    mavg = jnp.asarray((blk[:, None] == blk[None, :]).astype(np.float32) / HEAD_DIM, BF16)
    d_ff = w_ffn_down.shape[1]
    return dict(
        g1=norm1_g[l][None, :], w_in_t=packed,
        qg=jnp.tile(q_norm_g[l], N_HEADS)[None, :], kg=jnp.tile(k_norm_g[l], N_KV_HEADS)[None, :],
        kig=jnp.concatenate([kidx_norm_g[l], jnp.ones((LANES - IDX_DIM,), F32)])[None, :],
        wa2h=wa2h, wa2l=wa2l, ba=gla_ba[l][None, :], mavg=mavg,
        gon=gla_onorm_g[l][None, :], w_oa=w_oa[l].astype(BF16), w_ob=w_ob[l].astype(BF16),
        w_o=w_o[l].astype(BF16), n2=norm2_g[l][None, :],
        w_g=w_ffn_gu[l][:, :d_ff].astype(BF16), w_u=w_ffn_gu[l][:, d_ff:].astype(BF16),
        w_d=w_ffn_down[l].astype(BF16))


def _rope_table(pos):
    half = ROT_DIM // 2
    inv = jnp.power(ROPE_THETA, -jnp.arange(half, dtype=F32) * 2.0 / ROT_DIM)
    ang = pos.astype(F32)[:, None] * inv[None, :]
    cos, sin = jnp.cos(ang), jnp.sin(ang)
    n = pos.shape[0]
    rest = HEAD_DIM - ROT_DIM
    one = jnp.concatenate([cos, cos, jnp.ones((n, rest), F32)], axis=1)
    sa = jnp.concatenate([-sin, jnp.zeros((n, half + rest), F32)], axis=1)
    sb = jnp.concatenate([jnp.zeros((n, half), F32), sin, jnp.zeros((n, rest), F32)], axis=1)
    rep = LANES // HEAD_DIM
    return jnp.concatenate([jnp.tile(one, (1, rep)), jnp.tile(sa, (1, rep)), jnp.tile(sb, (1, rep))], axis=1)


def _tile_rows(n, want):
    t = min(n, want)
    while n % t:
        t //= 2
    return t


def _tail(x, attn_o, p, lw):
    t = x.shape[0]
    x1, hn = _merge(x, attn_o, p["gla_o"], p["gr"], p["ga"], p["gb"], lw, _tile_rows(t, 512))
    d_ff = lw["w_g"].shape[1]
    tf = d_ff // 2 if (d_ff // 2) % LANES == 0 else d_ff
    return _ffn(hn, x1, lw, _tile_rows(t, 512), tf)


def kernel(x_prompt, x_sample, cache_k, cache_v, cache_kidx, state_gla, page_table, norm1_g, w_in, q_norm_g,
           k_norm_g, kidx_norm_g, gla_wa2, gla_ba, gla_onorm_g, w_oa, w_ob, w_o, norm2_g, w_ffn_gu, w_ffn_down):
    n_b, seq, d_model = x_prompt.shape
    n_bd, t_d, _ = x_sample.shape
    depth = w_in.shape[0]
    n_pool, page = cache_k.shape[1], cache_k.shape[2]
    past_len = page_table.shape[1] * page
    topk_p = min(TOPK_MAX, seq // 4)
    topk_s = min(TOPK_MAX, (past_len + t_d) // 4)

    tab_p = _rope_table(jnp.arange(seq, dtype=I32))
    tab_s = jnp.tile(_rope_table(past_len + jnp.arange(t_d, dtype=I32)), (n_bd, 1))
    ckt = jnp.transpose(cache_k, (0, 1, 3, 4, 2)).reshape(depth, n_pool, KV_W, page)
    cvt = jnp.transpose(cache_v, (0, 1, 3, 4, 2)).reshape(depth, n_pool, KV_W, page)
    ckit = jnp.transpose(cache_kidx, (0, 1, 3, 2))

    hp = x_prompt.reshape(n_b * seq, d_model)
    hs = x_sample.reshape(n_bd * t_d, d_model)
    tm_p = _tile_rows(seq, 256)
    tm_s = _tile_rows(n_bd * t_d, 256)
    tq = _tile_rows(seq, 256)
    kc = _tile_rows(seq, 512)
    tb = _tile_rows(seq, 512)
    sub_p = _tile_rows(tb, GLA_SUB)
    group = _tile_rows(n_bd, SAMPLE_GROUP)
    outs = [[] for _ in range(8)]
    for l in range(depth):
        lw = _pack_layer(l, norm1_g, w_in, q_norm_g, k_norm_g, kidx_norm_g, gla_wa2, gla_ba, gla_onorm_g,
                         w_oa, w_ob, w_o, norm2_g, w_ffn_gu, w_ffn_down)

        p = _proj(hp, lw, tab_p, seq // tm_p, tm_p)
        attn_o = _attn_prompt(p, n_b, seq, topk_p, tq, kc)
        s0 = jnp.zeros((n_b, GLA_HEADS, GLA_DK, GLA_DV), F32)
        p["gla_o"], s_fin = _gla(p["gq"], p["gk"], p["gv"], p["la"], s0, n_b, seq, tb, sub_p)
        hp = _tail(hp, attn_o, p, lw)
        outs[0].append(p["k"].reshape(n_b, seq, N_KV_HEADS, HEAD_DIM))
        outs[1].append(p["v"].reshape(n_b, seq, N_KV_HEADS, HEAD_DIM))
        outs[2].append(p["ki"].reshape(n_b, seq, IDX_DIM))
        outs[3].append(s_fin)

        p = _proj(hs, lw, tab_s, (n_bd * t_d) // tm_s, tm_s)
        qi_r = p["qi"].reshape(n_bd, t_d, IDX_HEADS, IDX_DIM).transpose(0, 2, 1, 3)
        qi_r = qi_r.reshape(n_bd, IDX_HEADS * t_d, IDX_DIM)
        wi_r = p["wi"][:, IDX_DIM:IDX_DIM + IDX_HEADS].reshape(n_bd, t_d, IDX_HEADS).transpose(0, 2, 1)
        wi_r = wi_r.reshape(n_bd, IDX_HEADS * t_d, 1)
        q_r = p["q"].reshape(n_bd, t_d, N_KV_HEADS, KV_REP, HEAD_DIM).transpose(0, 2, 3, 1, 4)
        q_r = q_r.reshape(n_bd, N_KV_HEADS, KV_REP * t_d, HEAD_DIM)
        attn_o = _attn_sample(page_table, ckt, cvt, ckit, l, p["k"].reshape(n_bd, t_d, KV_W),
                              p["v"].reshape(n_bd, t_d, KV_W), p["ki"].reshape(n_bd, t_d, IDX_DIM),
                              qi_r, wi_r, q_r, topk_s, group)
        attn_o = attn_o.reshape(n_bd * t_d, ATTN_W)
        p["gla_o"], s_new = _gla(p["gq"], p["gk"], p["gv"], p["la"], state_gla[l], n_bd, t_d, t_d, t_d)
        hs = _tail(hs, attn_o, p, lw)
        outs[4].append(p["k"].reshape(n_bd, t_d, N_KV_HEADS, HEAD_DIM))
        outs[5].append(p["v"].reshape(n_bd, t_d, N_KV_HEADS, HEAD_DIM))
        outs[6].append(p["ki"].reshape(n_bd, t_d, IDX_DIM))
        outs[7].append(s_new)

    y_prompt = hp.reshape(n_b, seq, d_model)
    y_sample = hs.reshape(n_bd, t_d, d_model)
    return (y_prompt, y_sample) + tuple(jnp.stack(o) for o in outs)
```

```python
import functools

import numpy as np
import jax
import jax.numpy as jnp
from jax import lax
from jax.experimental import pallas as pl
from jax.experimental.pallas import tpu as pltpu

N_HEADS = 8
N_KV_HEADS = 4
HEAD_DIM = 64
ROT_DIM = HEAD_DIM // 4
ROPE_THETA = 500000.0
IDX_HEADS = 4
IDX_DIM = 64
IDX_ROT_DIM = IDX_DIM // 4
TOPK_MAX = 256
GLA_HEADS = 4
GLA_DK = 64
GLA_DV = 128
GLA_GATE_RANK = 16
GLA_TAU = 16.0
EPS = 1e-6

ATTN_W = N_HEADS * HEAD_DIM
KV_W = N_KV_HEADS * HEAD_DIM
KV_REP = N_HEADS // N_KV_HEADS
IDXQ_W = IDX_HEADS * IDX_DIM
GLA_K_W = GLA_HEADS * GLA_DK
GLA_V_W = GLA_HEADS * GLA_DV

LANES = 128
SUBLANES = 8
VMEM_LIMIT = 56 * 1024 * 1024
GLA_CHUNK = 64
GLA_SUB = 16
GLA_MAX_CHUNK_LOG_DECAY = 60.0
GLA_SEQS = 8
SAMPLE_GROUP = 8
NEG_BIG = -1e30
LOG2E = 1.4426950408889634
Q_SCALE = HEAD_DIM ** -0.5 * LOG2E
SHIFT_ROWS = 2
ONES_ROWS = 16
KEY_NEG_INF = -2139095041
INT_MIN = -2 ** 31

F32 = jnp.float32
BF16 = jnp.bfloat16
I32 = jnp.int32

_C_Q = 0
_C_K = _C_Q + ATTN_W
_C_V = _C_K + KV_W
_C_QI = _C_V + KV_W
_C_KIWI = _C_QI + IDXQ_W
_C_GQ = _C_KIWI + LANES
_C_GK = _C_GQ + GLA_K_W
_C_GV = _C_GK + GLA_K_W
_C_GA = _C_GV + GLA_V_W
_C_GR = _C_GA + LANES
NP_IN = _C_GR + GLA_V_W


def _cparams(sem):
    return pltpu.CompilerParams(dimension_semantics=sem, vmem_limit_bytes=VMEM_LIMIT)


def _split_hi_lo(x):
    hi = x.astype(BF16)
    lo = (x - hi.astype(F32)).astype(BF16)
    return hi, lo


def _split3(x):
    hi = x.astype(BF16)
    r = x - hi.astype(F32)
    mid = r.astype(BF16)
    return hi, mid, (r - mid.astype(F32)).astype(BF16)


def _dot(a, b):
    return jnp.dot(a, b, preferred_element_type=F32)


def _dot_nt(a, b):
    return lax.dot_general(a, b, (((1,), (1,)), ((), ())), preferred_element_type=F32)


def _dot_tn(a, b):
    return lax.dot_general(a, b, (((0,), (0,)), ((), ())), preferred_element_type=F32)


def _sigmoid(x):
    return 1.0 / (1.0 + jnp.exp(-x))


def _float_key(x):
    x = jnp.where(x == 0.0, 0.0, x)
    bits = pltpu.bitcast(x, I32)
    return bits ^ ((bits >> 31) & 0x7FFFFFFF)


def _group_mean_sq(h, m_ref):
    hi, lo = _split_hi_lo(h * h)
    w = h.shape[-1]
    m = m_ref[:w, :w]
    return _dot(hi, m) + _dot(lo, m)


def _rope(h, cos, sa, sb):
    w = h.shape[-1]
    return h * cos + pltpu.roll(h, w - ROT_DIM // 2, 1) * sa + pltpu.roll(h, ROT_DIM // 2, 1) * sb


def _proj_kernel(x_ref, g1_ref, w_ref, qg_ref, kg_ref, kig_ref, wa2h_ref, wa2l_ref, ba_ref, tab_ref, m_ref,
                 q_o, qt_o, k_o, v_o, kh_o, vt_o, qi_o, qit_o, ki_o, kib_o, wi_o, wit_o,
                 gq_o, gk_o, gv_o, la_o, gr_o, ga_o, gb_o):
    d_model = x_ref.shape[-1]
    x = x_ref[...]
    ms = jnp.mean(x * x, axis=-1, keepdims=True)
    xn = ((x * lax.rsqrt(ms + EPS)) * g1_ref[...]).astype(BF16)

    def cols(a, n):
        return _dot_nt(xn, w_ref[a:a + n, :])

    cos1, sa1, sb1 = tab_ref[:, :LANES], tab_ref[:, LANES:2 * LANES], tab_ref[:, 2 * LANES:]

    def tiled(t, w):
        return jnp.concatenate([t] * (w // LANES), axis=1) if w > LANES else t

    h = cols(_C_Q, ATTN_W)
    h = (h * lax.rsqrt(_group_mean_sq(h, m_ref) + EPS)) * qg_ref[...]
    h = _rope(h, tiled(cos1, ATTN_W), tiled(sa1, ATTN_W), tiled(sb1, ATTN_W)) * Q_SCALE
    q_o[...] = h.astype(BF16)
    qt_o[...] = h.T.astype(BF16)
    h = cols(_C_K, KV_W)
    h = (h * lax.rsqrt(_group_mean_sq(h, m_ref) + EPS)) * kg_ref[...]
    h = _rope(h, tiled(cos1, KV_W), tiled(sa1, KV_W), tiled(sb1, KV_W))
    k_o[...] = h
    lane64 = lax.broadcasted_iota(I32, (h.shape[0], HEAD_DIM), 1)
    shift_cols = jnp.where(lane64 < SHIFT_ROWS, 1.0, 0.0)
    for g in range(N_KV_HEADS):
        kh_o[g] = jnp.concatenate([h[:, g * HEAD_DIM:(g + 1) * HEAD_DIM], shift_cols], axis=1).astype(BF16)
    h = cols(_C_V, KV_W)
    v_o[...] = h
    ht = h.T
    for g in range(N_KV_HEADS):
        vt_o[g] = jnp.concatenate([ht[g * HEAD_DIM:(g + 1) * HEAD_DIM, :], jnp.ones((ONES_ROWS, ht.shape[1]), F32)],
                                  axis=0).astype(BF16)
    h = cols(_C_QI, IDXQ_W)
    h = _rope(h, tiled(cos1, IDXQ_W), tiled(sa1, IDXQ_W), tiled(sb1, IDXQ_W))
    qi_o[...] = h.astype(BF16)
    qit_o[...] = h.T.astype(BF16)
    h = cols(_C_KIWI, LANES)
    hw = h * (IDX_HEADS ** -0.5 * IDX_DIM ** -0.5)
    wi_o[...] = hw
    wit_o[...] = hw.T[IDX_DIM:IDX_DIM + SUBLANES, :]
    lane = lax.broadcasted_iota(I32, (1, LANES), 1)
    is_ki = lane < IDX_DIM
    hk = jnp.where(is_ki, h, 0.0)
    hk = (hk * lax.rsqrt(_group_mean_sq(hk, m_ref) + EPS)) * kig_ref[...]
    hk = _rope(hk, jnp.where(is_ki, cos1, 1.0), jnp.where(is_ki, sa1, 0.0), jnp.where(is_ki, sb1, 0.0))
    ki_o[...] = hk[:, :IDX_DIM]
    kib_o[...] = hk[:, :IDX_DIM].astype(BF16)
    gq_o[...] = cols(_C_GQ, GLA_K_W) * GLA_DK ** -0.5
    gk_o[...] = cols(_C_GK, GLA_K_W)
    gv_o[...] = cols(_C_GV, GLA_V_W)
    ga_hi, ga_lo = _split_hi_lo(cols(_C_GA, LANES))
    z = _dot(ga_hi, wa2h_ref[...]) + _dot(ga_lo, wa2h_ref[...]) + _dot(ga_hi, wa2l_ref[...]) + ba_ref[...]
    la_o[...] = (jnp.minimum(z, 0.0) - jnp.log1p(jnp.exp(-jnp.abs(z)))) / GLA_TAU
    gr_o[...] = cols(_C_GR, GLA_V_W)
    ga_o[...] = cols(NP_IN, d_model)
    gb_o[...] = cols(NP_IN + d_model, d_model)


_PROJ_NAMES = ("q", "qt", "k", "v", "kh", "vt", "qi", "qit", "ki", "kib", "wi", "wit",
               "gq", "gk", "gv", "la", "gr", "ga", "gb")


def _proj(x, lw, tab, tab_blocks, tm):
    t, d_model = x.shape
    row = lambda w: pl.BlockSpec((tm, w), lambda i: (i, 0))
    col = lambda w: pl.BlockSpec((w, tm), lambda i: (0, i))
    const = lambda a: pl.BlockSpec(a.shape, lambda i: (0,) * a.ndim)
    rsh = lambda w, dt: (row(w), jax.ShapeDtypeStruct((t, w), dt))
    csh = lambda w, dt: (col(w), jax.ShapeDtypeStruct((w, t), dt))
    outs = dict(
        q=rsh(ATTN_W, BF16), qt=csh(ATTN_W, BF16), k=rsh(KV_W, F32), v=rsh(KV_W, F32),
        kh=(pl.BlockSpec((N_KV_HEADS, tm, 2 * HEAD_DIM), lambda i: (0, i, 0)),
            jax.ShapeDtypeStruct((N_KV_HEADS, t, 2 * HEAD_DIM), BF16)),
        vt=(pl.BlockSpec((N_KV_HEADS, HEAD_DIM + ONES_ROWS, tm), lambda i: (0, 0, i)),
            jax.ShapeDtypeStruct((N_KV_HEADS, HEAD_DIM + ONES_ROWS, t), BF16)),
        qi=rsh(IDXQ_W, BF16), qit=csh(IDXQ_W, BF16), ki=rsh(IDX_DIM, F32),
        kib=rsh(IDX_DIM, BF16), wi=rsh(LANES, F32), wit=csh(SUBLANES, F32),
        gq=rsh(GLA_K_W, F32), gk=rsh(GLA_K_W, F32), gv=rsh(GLA_V_W, F32), la=rsh(GLA_K_W, F32),
        gr=rsh(GLA_V_W, F32), ga=rsh(d_model, F32), gb=rsh(d_model, F32))
    consts = [lw["g1"], lw["w_in_t"], lw["qg"], lw["kg"], lw["kig"], lw["wa2h"], lw["wa2l"], lw["ba"]]
    res = pl.pallas_call(
        _proj_kernel,
        grid=(t // tm,),
        in_specs=[row(d_model)] + [const(a) for a in consts]
        + [pl.BlockSpec((tm, 3 * LANES), lambda i: (i % tab_blocks, 0)), const(lw["mavg"])],
        out_specs=[outs[n][0] for n in _PROJ_NAMES],
        out_shape=[outs[n][1] for n in _PROJ_NAMES],
        compiler_params=_cparams(("parallel",)),
        name="proj",
    )(x, *consts, tab, lw["mavg"])
    return dict(zip(_PROJ_NAMES, res))


def _col_reduce8(x, op):
    r, c = x.shape
    blk = 8 * SUBLANES
    if r > blk and r % blk == 0:
        parts = x.reshape(r // blk, blk, c)
        x = parts[0]
        for j in range(1, r // blk):
            x = op(x, parts[j])
        r = blk
    while r > SUBLANES:
        r //= 2
        x = op(x[:r], x[r:])
    return x


def _topk_select_t(key_ref, nc, kc, tq, n_keys, topk):
    kf = float(topk)

    def count(pred):
        def body(c, acc):
            off = pl.multiple_of(c * kc, kc)
            kpos = off + lax.broadcasted_iota(I32, (kc, 1), 0)
            hit = pred(key_ref[pl.ds(off, kc), :], kpos)
            return acc + _col_reduce8(jnp.where(hit, 1.0, 0.0), jnp.add)
        acc = lax.fori_loop(0, nc, body, jnp.zeros((SUBLANES, tq), F32))
        return jnp.sum(acc, axis=0, keepdims=True)

    cur = jnp.where(count(lambda k, p: k >= 0) >= kf, 0, INT_MIN).astype(I32)

    def bit_body(j, cur):
        cand = cur | jnp.left_shift(jnp.int32(1), 30 - j)
        return jnp.where(count(lambda k, p: k >= cand) >= kf, cand, cur)

    thr = lax.fori_loop(0, 31, bit_body, cur)
    need = kf - count(lambda k, p: k > thr)
    excess = count(lambda k, p: k == thr) - need
    excess = jnp.where(thr <= KEY_NEG_INF, 0.0, excess)
    n_bits = max(1, int(np.ceil(np.log2(n_keys))))

    def find_cut():
        def body(j, v):
            cand = v | jnp.left_shift(jnp.int32(1), n_bits - 1 - j)
            c = count(lambda k, p: (k == thr) & (p < cand))
            return jnp.where(c < need, cand, v)
        return lax.fori_loop(0, n_bits, body, jnp.zeros((1, tq), I32))

    cut = lax.cond(jnp.max(excess) > 0.0, find_cut, lambda: jnp.full((1, tq), n_keys, I32))

    def mask_of(keys, kpos):
        return (keys > thr) | ((keys == thr) & (kpos <= cut))
    return mask_of


def _attn_prompt_kernel(qt_ref, qit_ref, wit_ref, kh_ref, vt_ref, kib_ref, o_ref,
                        key_ref, bias_ref, ot_ref, qa_ref, acc_ref,
                        *, topk, kc):
    tq = qt_ref.shape[1]
    n_keys = kib_ref.shape[0]
    i = pl.program_id(1)
    nc = ((i + 1) * tq + kc - 1) // kc
    qpos = i * tq + lax.broadcasted_iota(I32, (1, tq), 1)
    wit = wit_ref[...]

    def score_chunk(c, carry):
        off = pl.multiple_of(c * kc, kc)
        kic = kib_ref[pl.ds(off, kc), :]
        acc = jnp.zeros((kc, tq), F32)
        for h in range(IDX_HEADS):
            s = _dot(kic, qit_ref[h * IDX_DIM:(h + 1) * IDX_DIM, :])
            acc = acc + wit[h:h + 1, :] * jnp.maximum(s, 0.0)
        kpos = off + lax.broadcasted_iota(I32, (kc, 1), 0)
        key_ref[pl.ds(off, kc), :] = _float_key(jnp.where(kpos <= qpos, acc, -jnp.inf))
        return carry

    lax.fori_loop(0, nc, score_chunk, 0)
    mask_of = _topk_select_t(key_ref, nc, kc, tq, n_keys, topk)

    def bias_chunk(c, carry):
        off = pl.multiple_of(c * kc, kc)
        kpos = off + lax.broadcasted_iota(I32, (kc, 1), 0)
        sel = mask_of(key_ref[pl.ds(off, kc), :], kpos) & (kpos <= qpos)
        bias_ref[pl.ds(off, kc), :] = jnp.where(sel, 0.0, NEG_BIG)
        return carry

    lax.fori_loop(0, nc, bias_chunk, 0)

    row = lax.broadcasted_iota(I32, (2 * HEAD_DIM, 1), 0)

    def q_operand(g, m):
        heads = range(g * KV_REP, (g + 1) * KV_REP)
        qcat = jnp.concatenate([qt_ref[h * HEAD_DIM:(h + 1) * HEAD_DIM, :] for h in heads], axis=1).astype(F32)
        qa = jnp.concatenate([qcat, jnp.zeros_like(qcat)], axis=0)
        if m is not None:
            m_hi = m.astype(BF16).astype(F32)
            qa = jnp.where(row == HEAD_DIM, -m_hi, jnp.where(row == HEAD_DIM + 1, m_hi - m, qa))
        return qa.astype(BF16)

    def chunk_bias(c):
        off = pl.multiple_of(c * kc, kc)
        b = bias_ref[pl.ds(off, kc), :]
        return off, jnp.concatenate([b] * KV_REP, axis=1)

    def logits(g, off, b2):
        return _dot(kh_ref[g, pl.ds(off, kc), :], qa_ref[g]) + b2

    for g in range(N_KV_HEADS):
        qa_ref[g] = q_operand(g, None)

    def pass_max(c, mparts):
        off, b2 = chunk_bias(c)
        return tuple(jnp.maximum(mp, _col_reduce8(logits(g, off, b2), jnp.maximum)) for g, mp in enumerate(mparts))

    mparts = lax.fori_loop(0, nc, pass_max,
                           tuple(jnp.full((SUBLANES, KV_REP * tq), NEG_BIG, F32) for _ in range(N_KV_HEADS)))
    for g in range(N_KV_HEADS):
        qa_ref[g] = q_operand(g, jnp.max(mparts[g], axis=0, keepdims=True))
    acc_ref[...] = jnp.zeros_like(acc_ref)

    def pass_acc(c, carry):
        off, b2 = chunk_bias(c)
        for g in range(N_KV_HEADS):
            pb = jnp.exp2(logits(g, off, b2)).astype(BF16)
            vt = vt_ref[g, :, pl.ds(off, kc)]
            for r in range(KV_REP):
                acc_ref[g * KV_REP + r] += _dot(vt, pb[:, r * tq:(r + 1) * tq])
        return carry

    lax.fori_loop(0, nc, pass_acc, 0)
    for h in range(N_HEADS):
        ot_ref[h * HEAD_DIM:(h + 1) * HEAD_DIM, :] = acc_ref[h, :HEAD_DIM, :] / acc_ref[h, HEAD_DIM:HEAD_DIM + 1, :]

    o_ref[...] = ot_ref[...].T.astype(o_ref.dtype)


def _attn_prompt(p, n_seq, seq, topk, tq, kc):
    t = p["qt"].shape[1]
    nq = seq // tq
    qcol = lambda w: pl.BlockSpec((w, tq), lambda b, i: (0, b * nq + i))
    return pl.pallas_call(
        functools.partial(_attn_prompt_kernel, topk=topk, kc=kc),
        grid=(n_seq, nq),
        in_specs=[qcol(ATTN_W), qcol(IDXQ_W), qcol(SUBLANES),
                  pl.BlockSpec((N_KV_HEADS, seq, 2 * HEAD_DIM), lambda b, i: (0, b, 0)),
                  pl.BlockSpec((N_KV_HEADS, HEAD_DIM + ONES_ROWS, seq), lambda b, i: (0, 0, b)),
                  pl.BlockSpec((seq, IDX_DIM), lambda b, i: (b, 0))],
        out_specs=pl.BlockSpec((tq, ATTN_W), lambda b, i: (b * nq + i, 0)),
        out_shape=jax.ShapeDtypeStruct((t, ATTN_W), BF16),
        scratch_shapes=[pltpu.VMEM((seq, tq), I32), pltpu.VMEM((seq, tq), F32), pltpu.VMEM((ATTN_W, tq), F32),
                        pltpu.VMEM((N_KV_HEADS, 2 * HEAD_DIM, KV_REP * tq), BF16),
                        pltpu.VMEM((N_HEADS, HEAD_DIM + ONES_ROWS, tq), F32)],
        compiler_params=_cparams(("parallel", "parallel")),
        name="attn_prompt",
    )(p["qt"], p["qit"], p["wit"], p["kh"], p["vt"], p["kib"])


def _count_rows(mask):
    return jnp.sum(jnp.where(mask, 1.0, 0.0), axis=1, keepdims=True)


def _topk_select(key_ref, rows, n_keys, topk):
    kf = float(topk)
    cur = jnp.where(_count_rows(key_ref[...] >= 0) >= kf, 0, INT_MIN).astype(I32)

    def bit_body(j, cur):
        cand = cur | jnp.left_shift(jnp.int32(1), 30 - j)
        return jnp.where(_count_rows(key_ref[...] >= cand) >= kf, cand, cur)

    thr = lax.fori_loop(0, 31, bit_body, cur)
    need = kf - _count_rows(key_ref[...] > thr)
    excess = _count_rows(key_ref[...] == thr) - need
    excess = jnp.where(thr <= KEY_NEG_INF, 0.0, excess)
    n_bits = max(1, int(np.ceil(np.log2(n_keys))))

    def find_cut():
        def body(j, v):
            cand = v | jnp.left_shift(jnp.int32(1), n_bits - 1 - j)
            pos = lax.broadcasted_iota(I32, (rows, n_keys), 1)
            c = _count_rows((key_ref[...] == thr) & (pos < cand))
            return jnp.where(c < need, cand, v)
        return lax.fori_loop(0, n_bits, body, jnp.zeros((rows, 1), I32))

    cut = lax.cond(jnp.max(excess) > 0.0, find_cut, lambda: jnp.full((rows, 1), n_keys, I32))

    def mask_of(keys, pos):
        return (keys > thr) | ((keys == thr) & (pos <= cut))
    return mask_of


def _attn_sample_kernel(pt_ref, *refs, topk, past_len, group):
    ck, cv, cki = refs[:group], refs[group:2 * group], refs[2 * group:3 * group]
    (kn_ref, vn_ref, kin_ref, qi_ref, wi_ref, q_ref, o_ref,
     ks_ref, vs_ref, kis_ref, key_ref, bias_ref) = refs[3 * group:]
    p = pl.program_id(1)
    page = ck[0].shape[1]
    t_new = kn_ref.shape[1]
    n_keys = past_len + LANES
    rows = group * t_new
    off = pl.multiple_of(p * page, page)
    for j in range(group):
        ks_ref[j, :, pl.ds(off, page)] = ck[j][...].astype(BF16)
        vs_ref[j, :, pl.ds(off, page)] = cv[j][...].astype(BF16)
        kis_ref[j, :, pl.ds(off, page)] = cki[j][...].astype(BF16)

    @pl.when(p == pl.num_programs(1) - 1)
    def _():
        def padded(new):
            pad = jnp.zeros((LANES - t_new, new.shape[1]), F32)
            return jnp.concatenate([new, pad], axis=0).astype(BF16)

        kpos = lax.broadcasted_iota(I32, (1, n_keys), 1)
        qpos1 = past_len + lax.broadcasted_iota(I32, (t_new, 1), 0)
        for j in range(group):
            qi = qi_ref[j]
            s = jnp.concatenate([_dot(qi, kis_ref[j]), _dot_nt(qi, padded(kin_ref[j]))], axis=1)
            s = wi_ref[j] * jnp.maximum(s, 0.0)
            score = s[0:t_new]
            for h in range(1, IDX_HEADS):
                score = score + s[h * t_new:(h + 1) * t_new]
            key_ref[j * t_new:(j + 1) * t_new, :] = _float_key(jnp.where(kpos <= qpos1, score, -jnp.inf))

        mask_of = _topk_select(key_ref, rows, n_keys, topk)
        qpos = past_len + lax.rem(lax.broadcasted_iota(I32, (rows, 1), 0), t_new)
        bias_ref[...] = jnp.where(mask_of(key_ref[...], kpos) & (kpos <= qpos), 0.0, NEG_BIG)

        for j in range(group):
            b1 = bias_ref[j * t_new:(j + 1) * t_new, :]
            bias = jnp.concatenate([b1] * KV_REP, axis=0)
            kn = padded(kn_ref[j])
            vn = padded(vn_ref[j])
            for g in range(N_KV_HEADS):
                gs = slice(g * HEAD_DIM, (g + 1) * HEAD_DIM)
                qg = q_ref[j, g]
                s = jnp.concatenate([_dot(qg, ks_ref[j, gs, :]), _dot_nt(qg, kn[:, gs])], axis=1) + bias
                m = jnp.max(s, axis=1, keepdims=True)
                pr = jnp.exp2(s - m)
                l = jnp.sum(pr, axis=1, keepdims=True)
                pb = pr.astype(BF16)
                o = (_dot_nt(pb[:, :past_len], vs_ref[j, gs, :]) + _dot(pb[:, past_len:], vn[:, gs])) / l
                for r in range(KV_REP):
                    h = g * KV_REP + r
                    o_ref[j, :, h * HEAD_DIM:(h + 1) * HEAD_DIM] = o[r * t_new:(r + 1) * t_new].astype(o_ref.dtype)


def _attn_sample(page_table, ckt, cvt, ckit, layer, k_new, v_new, ki_new, qi_r, wi_r, q_r, topk, group):
    n_seq, n_pages = page_table.shape
    page = ckt.shape[3]
    past_len = n_pages * page
    t_new = k_new.shape[1]
    n_keys = past_len + LANES
    rows = group * t_new
    pt = page_table.reshape(-1)

    def cspec(w, j):
        return pl.BlockSpec((None, None, w, page),
                            lambda b, p, pt: (layer, pt[(b * group + j) * n_pages + p], 0, 0))

    gspec = lambda *s: pl.BlockSpec((group,) + s, lambda b, p, pt: (b,) + (0,) * len(s))
    return pl.pallas_call(
        functools.partial(_attn_sample_kernel, topk=topk, past_len=past_len, group=group),
        grid_spec=pltpu.PrefetchScalarGridSpec(
            num_scalar_prefetch=1,
            grid=(n_seq // group, n_pages),
            in_specs=[cspec(KV_W, j) for j in range(group)] + [cspec(KV_W, j) for j in range(group)]
            + [cspec(IDX_DIM, j) for j in range(group)]
            + [gspec(t_new, KV_W), gspec(t_new, KV_W), gspec(t_new, IDX_DIM),
               gspec(IDX_HEADS * t_new, IDX_DIM), gspec(IDX_HEADS * t_new, 1),
               gspec(N_KV_HEADS, KV_REP * t_new, HEAD_DIM)],
            out_specs=gspec(t_new, ATTN_W),
            scratch_shapes=[pltpu.VMEM((group, KV_W, past_len), BF16), pltpu.VMEM((group, KV_W, past_len), BF16),
                            pltpu.VMEM((group, IDX_DIM, past_len), BF16),
                            pltpu.VMEM((rows, n_keys), I32), pltpu.VMEM((rows, n_keys), F32)]),
        out_shape=jax.ShapeDtypeStruct((n_seq, t_new, ATTN_W), BF16),
        compiler_params=_cparams(("parallel", "arbitrary")),
        name="attn_sample",
    )(pt, *([ckt] * group), *([cvt] * group), *([ckit] * group), k_new, v_new, ki_new, qi_r, wi_r, q_r)


def _gla_span(q_ref, k_ref, v_ref, la_ref, o_ref, st_ref, s, r0, size):
    rows = pl.ds(r0, size)
    r = lax.broadcasted_iota(I32, (size, size), 0)
    c = lax.broadcasted_iota(I32, (size, size), 1)
    causal = r >= c
    tril = jnp.where(causal, 1.0, 0.0).astype(BF16)
    eye = (lax.broadcasted_iota(I32, (GLA_DK, GLA_DK), 0) == lax.broadcasted_iota(I32, (GLA_DK, GLA_DK), 1))
    b = sum(_dot(tril, part) for part in _split3(la_ref[s, rows, :]))
    b_end = b[size - 1:size, :]
    b_mid = b[size // 2 - 1:size // 2, :]
    q = q_ref[s, rows, :]
    k = k_ref[s, rows, :]
    v = v_ref[s, rows, :].astype(BF16)
    q_in = (q * jnp.exp(b)).astype(BF16)
    q_d = (q * jnp.exp(b - b_mid)).astype(BF16)
    k_d = (k * jnp.exp(b_mid - b)).astype(BF16)
    k_out = (k * jnp.exp(b_end - b)).astype(BF16)
    for h in range(GLA_HEADS):
        ks = slice(h * GLA_DK, (h + 1) * GLA_DK)
        vs = slice(h * GLA_DV, (h + 1) * GLA_DV)
        dec = jnp.exp(jnp.sum(jnp.where(eye, b_end[:, ks], 0.0), axis=1, keepdims=True))
        st = st_ref[s, h]
        a = jnp.where(causal, _dot_nt(q_d[:, ks], k_d[:, ks]), 0.0).astype(BF16)
        o_ref[s, rows, vs] = _dot(q_in[:, ks], st.astype(BF16)) + _dot(a, v[:, vs])
        st_ref[s, h] = st * dec + _dot_tn(k_out[:, ks], v[:, vs])


def _gla_kernel(q_ref, k_ref, v_ref, la_ref, s0_ref, o_ref, sf_ref, st_ref, *, chunk, sub):
    ns, tb = q_ref.shape[0], q_ref.shape[1]
    span = functools.partial(_gla_span, q_ref, k_ref, v_ref, la_ref, o_ref, st_ref)

    @pl.when(pl.program_id(1) == 0)
    def _():
        st_ref[...] = s0_ref[...]

    def body(i, carry):
        r0 = pl.multiple_of(i * chunk, chunk)
        if chunk == sub:
            for s in range(ns):
                span(s, r0, chunk)
            return carry
        tot = [jnp.sum(la_ref[s, pl.ds(r0, chunk), :], axis=0, keepdims=True) for s in range(ns)]
        mild = jnp.min(jnp.concatenate(tot, axis=0)) > -GLA_MAX_CHUNK_LOG_DECAY

        @pl.when(mild)
        def _():
            for s in range(ns):
                span(s, r0, chunk)

        @pl.when(jnp.logical_not(mild))
        def _():
            for s in range(ns):
                for j in range(chunk // sub):
                    span(s, r0 + j * sub, sub)
        return carry

    lax.fori_loop(0, tb // chunk, body, 0)
    sf_ref[...] = st_ref[...]


def _gla(gq, gk, gv, la, s0, n_seq, seq, ns, tb, chunk, sub):
    nt = seq // tb
    r3 = lambda a: a.reshape(n_seq, seq, a.shape[-1])
    row = lambda w: pl.BlockSpec((ns, tb, w), lambda b, i: (b, i, 0))
    sspec = pl.BlockSpec((ns, GLA_HEADS, GLA_DK, GLA_DV), lambda b, i: (b, 0, 0, 0))
    o, s_fin = pl.pallas_call(
        functools.partial(_gla_kernel, chunk=chunk, sub=sub),
        grid=(n_seq // ns, nt),
        in_specs=[row(GLA_K_W), row(GLA_K_W), row(GLA_V_W), row(GLA_K_W), sspec],
        out_specs=[row(GLA_V_W), sspec],
        out_shape=[jax.ShapeDtypeStruct((n_seq, seq, GLA_V_W), F32),
                   jax.ShapeDtypeStruct((n_seq, GLA_HEADS, GLA_DK, GLA_DV), F32)],
        scratch_shapes=[pltpu.VMEM((ns, GLA_HEADS, GLA_DK, GLA_DV), F32)],
        compiler_params=_cparams(("parallel", "arbitrary")),
        name="gla",
    )(r3(gq), r3(gk), r3(gv), r3(la), s0)
    return o.reshape(n_seq * seq, GLA_V_W), s_fin


def _merge_kernel(x_ref, at_ref, gl_ref, gr_ref, ga_ref, gb_ref, gon_ref, woa_ref, wob_ref, wo_ref, n2_ref,
                  x1_o, hn_o):
    gl = gl_ref[...]
    gr = gr_ref[...]
    parts = []
    for h in range(GLA_HEADS):
        gh = gl[:, h * GLA_DV:(h + 1) * GLA_DV]
        ms = jnp.mean(gh * gh, axis=-1, keepdims=True)
        parts.append((gh * lax.rsqrt(ms + EPS)) * gon_ref[...])
    g_out = jnp.concatenate(parts, axis=1) * (gr * _sigmoid(gr))
    merged = (_sigmoid(ga_ref[...]) * _dot(at_ref[...], woa_ref[...])
              + _sigmoid(gb_ref[...]) * _dot(g_out.astype(BF16), wob_ref[...]))
    x1 = x_ref[...] + _dot(merged.astype(BF16), wo_ref[...])
    x1_o[...] = x1
    ms = jnp.mean(x1 * x1, axis=-1, keepdims=True)
    hn_o[...] = ((x1 * lax.rsqrt(ms + EPS)) * n2_ref[...]).astype(BF16)


def _merge(x, attn_o, gla_o, gr, gate_a, gate_b, lw, tm):
    t, d_model = x.shape
    row = lambda w: pl.BlockSpec((tm, w), lambda i: (i, 0))
    const = lambda a: pl.BlockSpec(a.shape, lambda i: (0,) * a.ndim)
    consts = [lw["gon"], lw["w_oa"], lw["w_ob"], lw["w_o"], lw["n2"]]
    return pl.pallas_call(
        _merge_kernel,
        grid=(t // tm,),
        in_specs=[row(d_model), row(ATTN_W), row(GLA_V_W), row(GLA_V_W), row(d_model), row(d_model)]
        + [const(a) for a in consts],
        out_specs=[row(d_model), row(d_model)],
        out_shape=[jax.ShapeDtypeStruct((t, d_model), F32), jax.ShapeDtypeStruct((t, d_model), BF16)],
        compiler_params=_cparams(("parallel",)),
        name="merge",
    )(x, attn_o, gla_o, gr, gate_a, gate_b, *consts)


def _ffn_kernel(hn_ref, x1_ref, wg_ref, wu_ref, wd_ref, y_ref, acc_ref):
    f = pl.program_id(1)

    @pl.when(f == 0)
    def _():
        acc_ref[...] = jnp.zeros_like(acc_ref)

    hn = hn_ref[...]
    gt = _dot(hn, wg_ref[...])
    up = _dot(hn, wu_ref[...])
    acc_ref[...] += _dot(((gt * _sigmoid(gt)) * up).astype(BF16), wd_ref[...])

    @pl.when(f == pl.num_programs(1) - 1)
    def _():
        y_ref[...] = x1_ref[...] + acc_ref[...]


def _ffn(hn, x1, lw, tm, tf):
    t, d_model = x1.shape
    d_ff = lw["w_g"].shape[1]
    row = pl.BlockSpec((tm, d_model), lambda i, f: (i, 0))
    return pl.pallas_call(
        _ffn_kernel,
        grid=(t // tm, d_ff // tf),
        in_specs=[row, row,
                  pl.BlockSpec((d_model, tf), lambda i, f: (0, f)),
                  pl.BlockSpec((d_model, tf), lambda i, f: (0, f)),
                  pl.BlockSpec((tf, d_model), lambda i, f: (f, 0))],
        out_specs=row,
        out_shape=jax.ShapeDtypeStruct((t, d_model), F32),
        scratch_shapes=[pltpu.VMEM((tm, d_model), F32)],
        compiler_params=_cparams(("parallel", "arbitrary")),
        name="ffn",
    )(hn, x1, lw["w_g"], lw["w_u"], lw["w_d"])


def _pack_layer(l, norm1_g, w_in, q_norm_g, k_norm_g, kidx_norm_g, gla_wa2, gla_ba, gla_onorm_g,
                w_oa, w_ob, w_o, norm2_g, w_ffn_gu, w_ffn_down):
    d_model = w_in.shape[1]
    wt = jnp.transpose(w_in, (2, 0, 1))[:, l, :]
    splits = (ATTN_W, KV_W, KV_W, IDXQ_W, IDX_DIM, IDX_HEADS, GLA_K_W, GLA_K_W, GLA_V_W, GLA_GATE_RANK,
              GLA_V_W, d_model, d_model)
    offs = np.cumsum(splits)[:-1].tolist()
    q, k, v, qi, ki, wi, gq, gk, gv, ga, gr, gate_a, gate_b = jnp.split(wt, offs, axis=0)
    z = lambda n: jnp.zeros((n, d_model), wt.dtype)
    packed = jnp.concatenate(
        [q, k, v, qi, ki, wi, z(LANES - IDX_DIM - IDX_HEADS), gq, gk, gv, ga, z(LANES - GLA_GATE_RANK),
         gr, gate_a, gate_b], axis=0).astype(BF16)
    wa2 = jnp.concatenate([gla_wa2[l], jnp.zeros((LANES - GLA_GATE_RANK, GLA_K_W), F32)], axis=0)
    wa2h, wa2l = _split_hi_lo(wa2)
    blk = np.arange(ATTN_W) // HEAD_DIM
    mavg = jnp.asarray((blk[:, None] == blk[None, :]).astype(np.float32) / HEAD_DIM, BF16)
    d_ff = w_ffn_down.shape[1]
    return dict(
        g1=norm1_g[l][None, :], w_in_t=packed,
        qg=jnp.tile(q_norm_g[l], N_HEADS)[None, :], kg=jnp.tile(k_norm_g[l], N_KV_HEADS)[None, :],
        kig=jnp.concatenate([kidx_norm_g[l], jnp.ones((LANES - IDX_DIM,), F32)])[None, :],
        wa2h=wa2h, wa2l=wa2l, ba=gla_ba[l][None, :], mavg=mavg,
        gon=gla_onorm_g[l][None, :], w_oa=w_oa[l].astype(BF16), w_ob=w_ob[l].astype(BF16),
        w_o=w_o[l].astype(BF16), n2=norm2_g[l][None, :],
        w_g=w_ffn_gu[l][:, :d_ff].astype(BF16), w_u=w_ffn_gu[l][:, d_ff:].astype(BF16),
        w_d=w_ffn_down[l].astype(BF16))


def _rope_table(pos):
    half = ROT_DIM // 2
    inv = jnp.power(ROPE_THETA, -jnp.arange(half, dtype=F32) * 2.0 / ROT_DIM)
    ang = pos.astype(F32)[:, None] * inv[None, :]
    cos, sin = jnp.cos(ang), jnp.sin(ang)
    n = pos.shape[0]
    rest = HEAD_DIM - ROT_DIM
    one = jnp.concatenate([cos, cos, jnp.ones((n, rest), F32)], axis=1)
    sa = jnp.concatenate([-sin, jnp.zeros((n, half + rest), F32)], axis=1)
    sb = jnp.concatenate([jnp.zeros((n, half), F32), sin, jnp.zeros((n, rest), F32)], axis=1)
    rep = LANES // HEAD_DIM
    return jnp.concatenate([jnp.tile(one, (1, rep)), jnp.tile(sa, (1, rep)), jnp.tile(sb, (1, rep))], axis=1)


def _tile_rows(n, want):
    t = min(n, want)
    while n % t:
        t //= 2
    return t


def _tail(x, attn_o, p, lw):
    t = x.shape[0]
    x1, hn = _merge(x, attn_o, p["gla_o"], p["gr"], p["ga"], p["gb"], lw, _tile_rows(t, 512))
    d_ff = lw["w_g"].shape[1]
    tf = d_ff // 2 if (d_ff // 2) % LANES == 0 else d_ff
    return _ffn(hn, x1, lw, _tile_rows(t, 512), tf)


def kernel(x_prompt, x_sample, cache_k, cache_v, cache_kidx, state_gla, page_table, norm1_g, w_in, q_norm_g,
           k_norm_g, kidx_norm_g, gla_wa2, gla_ba, gla_onorm_g, w_oa, w_ob, w_o, norm2_g, w_ffn_gu, w_ffn_down):
    n_b, seq, d_model = x_prompt.shape
    n_bd, t_d, _ = x_sample.shape
    depth = w_in.shape[0]
    n_pool, page = cache_k.shape[1], cache_k.shape[2]
    past_len = page_table.shape[1] * page
    topk_p = min(TOPK_MAX, seq // 4)
    topk_s = min(TOPK_MAX, (past_len + t_d) // 4)

    tab_p = _rope_table(jnp.arange(seq, dtype=I32))
    tab_s = jnp.tile(_rope_table(past_len + jnp.arange(t_d, dtype=I32)), (n_bd, 1))
    ckt = jnp.transpose(cache_k, (0, 1, 3, 4, 2)).reshape(depth, n_pool, KV_W, page)
    cvt = jnp.transpose(cache_v, (0, 1, 3, 4, 2)).reshape(depth, n_pool, KV_W, page)
    ckit = jnp.transpose(cache_kidx, (0, 1, 3, 2))

    hp = x_prompt.reshape(n_b * seq, d_model)
    hs = x_sample.reshape(n_bd * t_d, d_model)
    tm_p = _tile_rows(seq, 256)
    tm_s = _tile_rows(n_bd * t_d, 256)
    tq = _tile_rows(seq, 256)
    kc = _tile_rows(seq, 512)
    tb = _tile_rows(seq, 512)
    chunk_p = _tile_rows(tb, GLA_CHUNK)
    sub_p = _tile_rows(chunk_p, GLA_SUB)
    group = _tile_rows(n_bd, SAMPLE_GROUP)
    outs = [[] for _ in range(8)]
    for l in range(depth):
        lw = _pack_layer(l, norm1_g, w_in, q_norm_g, k_norm_g, kidx_norm_g, gla_wa2, gla_ba, gla_onorm_g,
                         w_oa, w_ob, w_o, norm2_g, w_ffn_gu, w_ffn_down)

        p = _proj(hp, lw, tab_p, seq // tm_p, tm_p)
        attn_o = _attn_prompt(p, n_b, seq, topk_p, tq, kc)
        s0 = jnp.zeros((n_b, GLA_HEADS, GLA_DK, GLA_DV), F32)
        p["gla_o"], s_fin = _gla(p["gq"], p["gk"], p["gv"], p["la"], s0, n_b, seq, _tile_rows(n_b, GLA_SEQS), tb,
                                 chunk_p, sub_p)
        hp = _tail(hp, attn_o, p, lw)
        outs[0].append(p["k"].reshape(n_b, seq, N_KV_HEADS, HEAD_DIM))
        outs[1].append(p["v"].reshape(n_b, seq, N_KV_HEADS, HEAD_DIM))
        outs[2].append(p["ki"].reshape(n_b, seq, IDX_DIM))
        outs[3].append(s_fin)

        p = _proj(hs, lw, tab_s, (n_bd * t_d) // tm_s, tm_s)
        qi_r = p["qi"].reshape(n_bd, t_d, IDX_HEADS, IDX_DIM).transpose(0, 2, 1, 3)
        qi_r = qi_r.reshape(n_bd, IDX_HEADS * t_d, IDX_DIM)
        wi_r = p["wi"][:, IDX_DIM:IDX_DIM + IDX_HEADS].reshape(n_bd, t_d, IDX_HEADS).transpose(0, 2, 1)
        wi_r = wi_r.reshape(n_bd, IDX_HEADS * t_d, 1)
        q_r = p["q"].reshape(n_bd, t_d, N_KV_HEADS, KV_REP, HEAD_DIM).transpose(0, 2, 3, 1, 4)
        q_r = q_r.reshape(n_bd, N_KV_HEADS, KV_REP * t_d, HEAD_DIM)
        attn_o = _attn_sample(page_table, ckt, cvt, ckit, l, p["k"].reshape(n_bd, t_d, KV_W),
                              p["v"].reshape(n_bd, t_d, KV_W), p["ki"].reshape(n_bd, t_d, IDX_DIM),
                              qi_r, wi_r, q_r, topk_s, group)
        attn_o = attn_o.reshape(n_bd * t_d, ATTN_W)
        p["gla_o"], s_new = _gla(p["gq"], p["gk"], p["gv"], p["la"], state_gla[l], n_bd, t_d,
                                 _tile_rows(n_bd, GLA_SEQS), t_d, t_d, t_d)
        hs = _tail(hs, attn_o, p, lw)
        outs[4].append(p["k"].reshape(n_bd, t_d, N_KV_HEADS, HEAD_DIM))
        outs[5].append(p["v"].reshape(n_bd, t_d, N_KV_HEADS, HEAD_DIM))
        outs[6].append(p["ki"].reshape(n_bd, t_d, IDX_DIM))
        outs[7].append(s_new)

    y_prompt = hp.reshape(n_b, seq, d_model)
    y_sample = hs.reshape(n_bd, t_d, d_model)
    return (y_prompt, y_sample) + tuple(jnp.stack(o) for o in outs)
```

```python
import functools

import numpy as np
import jax
import jax.numpy as jnp
from jax import lax
from jax.experimental import pallas as pl
from jax.experimental.pallas import tpu as pltpu

N_HEADS = 8
N_KV_HEADS = 4
HEAD_DIM = 64
ROT_DIM = HEAD_DIM // 4
ROPE_THETA = 500000.0
IDX_HEADS = 4
IDX_DIM = 64
IDX_ROT_DIM = IDX_DIM // 4
TOPK_MAX = 256
GLA_HEADS = 4
GLA_DK = 64
GLA_DV = 128
GLA_GATE_RANK = 16
GLA_TAU = 16.0
EPS = 1e-6

ATTN_W = N_HEADS * HEAD_DIM
KV_W = N_KV_HEADS * HEAD_DIM
KV_REP = N_HEADS // N_KV_HEADS
IDXQ_W = IDX_HEADS * IDX_DIM
GLA_K_W = GLA_HEADS * GLA_DK
GLA_V_W = GLA_HEADS * GLA_DV

LANES = 128
SUBLANES = 8
VMEM_LIMIT = 56 * 1024 * 1024
GLA_CHUNK = 64
GLA_SUB = 16
GLA_MAX_CHUNK_LOG_DECAY = 60.0
GLA_SEQS = 8
SAMPLE_GROUP = 8
PAGES_PER_STEP = 4
NEG_BIG = -1e30
LOG2E = 1.4426950408889634
Q_SCALE = HEAD_DIM ** -0.5 * LOG2E
SHIFT_ROWS = 2
ONES_ROWS = 16
KEY_NEG_INF = -2139095041
INT_MIN = -2 ** 31
INT16_MIN = -2 ** 15

F32 = jnp.float32
BF16 = jnp.bfloat16
I32 = jnp.int32
I16 = jnp.int16

_C_Q = 0
_C_K = _C_Q + ATTN_W
_C_V = _C_K + KV_W
_C_QI = _C_V + KV_W
_C_KIWI = _C_QI + IDXQ_W
_C_GQ = _C_KIWI + LANES
_C_GK = _C_GQ + GLA_K_W
_C_GV = _C_GK + GLA_K_W
_C_GA = _C_GV + GLA_V_W
_C_GR = _C_GA + LANES
NP_IN = _C_GR + GLA_V_W


def _cparams(sem):
    return pltpu.CompilerParams(dimension_semantics=sem, vmem_limit_bytes=VMEM_LIMIT)


def _split_hi_lo(x):
    hi = x.astype(BF16)
    lo = (x - hi.astype(F32)).astype(BF16)
    return hi, lo


def _split3(x):
    hi = x.astype(BF16)
    r = x - hi.astype(F32)
    mid = r.astype(BF16)
    return hi, mid, (r - mid.astype(F32)).astype(BF16)


def _dot(a, b):
    return jnp.dot(a, b, preferred_element_type=F32)


def _dot_nt(a, b):
    return lax.dot_general(a, b, (((1,), (1,)), ((), ())), preferred_element_type=F32)


def _dot_tn(a, b):
    return lax.dot_general(a, b, (((0,), (0,)), ((), ())), preferred_element_type=F32)


def _sigmoid(x):
    return 1.0 / (1.0 + jnp.exp(-x))


def _float_key(x):
    x = jnp.where(x == 0.0, 0.0, x)
    bits = pltpu.bitcast(x, I32)
    return bits ^ ((bits >> 31) & 0x7FFFFFFF)


def _group_mean_sq(h, m_ref):
    hi, lo = _split_hi_lo(h * h)
    w = h.shape[-1]
    m = m_ref[:w, :w]
    return _dot(hi, m) + _dot(lo, m)


def _rope(h, cos, sa, sb):
    w = h.shape[-1]
    return h * cos + pltpu.roll(h, w - ROT_DIM // 2, 1) * sa + pltpu.roll(h, ROT_DIM // 2, 1) * sb


def _proj_kernel(x_ref, g1_ref, w_ref, qg_ref, kg_ref, kig_ref, wa2h_ref, wa2l_ref, ba_ref, tab_ref, m_ref,
                 q_o, qt_o, k_o, v_o, kh_o, vt_o, qi_o, qit_o, ki_o, kib_o, wi_o, wit_o,
                 gq_o, gk_o, gv_o, la_o, gr_o, ga_o, gb_o, *, seq_minor):
    d_model = x_ref.shape[-1]
    x = x_ref[...]
    ms = jnp.mean(x * x, axis=-1, keepdims=True)
    xn = ((x * lax.rsqrt(ms + EPS)) * g1_ref[...]).astype(BF16)

    def cols(a, n):
        return _dot_nt(xn, w_ref[a:a + n, :])

    cos1, sa1, sb1 = tab_ref[:, :LANES], tab_ref[:, LANES:2 * LANES], tab_ref[:, 2 * LANES:]

    def tiled(t, w):
        return jnp.concatenate([t] * (w // LANES), axis=1) if w > LANES else t

    h = cols(_C_Q, ATTN_W)
    h = (h * lax.rsqrt(_group_mean_sq(h, m_ref) + EPS)) * qg_ref[...]
    h = _rope(h, tiled(cos1, ATTN_W), tiled(sa1, ATTN_W), tiled(sb1, ATTN_W)) * Q_SCALE
    q_o[...] = h.astype(BF16)
    qt_o[...] = h.T.astype(BF16)
    h = cols(_C_K, KV_W)
    h = (h * lax.rsqrt(_group_mean_sq(h, m_ref) + EPS)) * kg_ref[...]
    h = _rope(h, tiled(cos1, KV_W), tiled(sa1, KV_W), tiled(sb1, KV_W))
    k_o[...] = h.T if seq_minor else h
    lane64 = lax.broadcasted_iota(I32, (h.shape[0], HEAD_DIM), 1)
    shift_cols = jnp.where(lane64 < SHIFT_ROWS, 1.0, 0.0)
    for g in range(N_KV_HEADS):
        kh_o[g] = jnp.concatenate([h[:, g * HEAD_DIM:(g + 1) * HEAD_DIM], shift_cols], axis=1).astype(BF16)
    h = cols(_C_V, KV_W)
    ht = h.T
    v_o[...] = ht if seq_minor else h
    for g in range(N_KV_HEADS):
        vt_o[g] = jnp.concatenate([ht[g * HEAD_DIM:(g + 1) * HEAD_DIM, :], jnp.ones((ONES_ROWS, ht.shape[1]), F32)],
                                  axis=0).astype(BF16)
    h = cols(_C_QI, IDXQ_W)
    h = _rope(h, tiled(cos1, IDXQ_W), tiled(sa1, IDXQ_W), tiled(sb1, IDXQ_W))
    qi_o[...] = h.astype(BF16)
    qit_o[...] = h.T.astype(BF16)
    h = cols(_C_KIWI, LANES)
    hw = h * (IDX_HEADS ** -0.5 * IDX_DIM ** -0.5)
    wi_o[...] = hw
    wit_o[...] = hw.T[IDX_DIM:IDX_DIM + SUBLANES, :]
    lane = lax.broadcasted_iota(I32, (1, LANES), 1)
    is_ki = lane < IDX_DIM
    hk = jnp.where(is_ki, h, 0.0)
    hk = (hk * lax.rsqrt(_group_mean_sq(hk, m_ref) + EPS)) * kig_ref[...]
    hk = _rope(hk, jnp.where(is_ki, cos1, 1.0), jnp.where(is_ki, sa1, 0.0), jnp.where(is_ki, sb1, 0.0))
    ki_o[...] = hk.T[:IDX_DIM, :] if seq_minor else hk[:, :IDX_DIM]
    kib_o[...] = hk[:, :IDX_DIM].astype(BF16)
    gq_o[...] = cols(_C_GQ, GLA_K_W) * GLA_DK ** -0.5
    gk_o[...] = cols(_C_GK, GLA_K_W)
    gv_o[...] = cols(_C_GV, GLA_V_W)
    ga_hi, ga_lo = _split_hi_lo(cols(_C_GA, LANES))
    z = _dot(ga_hi, wa2h_ref[...]) + _dot(ga_lo, wa2h_ref[...]) + _dot(ga_hi, wa2l_ref[...]) + ba_ref[...]
    la_o[...] = (jnp.minimum(z, 0.0) - jnp.log1p(jnp.exp(-jnp.abs(z)))) / GLA_TAU
    gr_o[...] = cols(_C_GR, GLA_V_W)
    ga_o[...] = cols(NP_IN, d_model)
    gb_o[...] = cols(NP_IN + d_model, d_model)


_PROJ_NAMES = ("q", "qt", "k", "v", "kh", "vt", "qi", "qit", "ki", "kib", "wi", "wit",
               "gq", "gk", "gv", "la", "gr", "ga", "gb")


def _proj(x, lw, tab, tab_blocks, tm, seq_minor):
    t, d_model = x.shape
    row = lambda w: pl.BlockSpec((tm, w), lambda i: (i, 0))
    col = lambda w: pl.BlockSpec((w, tm), lambda i: (0, i))
    const = lambda a: pl.BlockSpec(a.shape, lambda i: (0,) * a.ndim, pipeline_mode=pl.Buffered(1))
    rsh = lambda w, dt: (row(w), jax.ShapeDtypeStruct((t, w), dt))
    csh = lambda w, dt: (col(w), jax.ShapeDtypeStruct((w, t), dt))

    def ret(w):
        if not seq_minor:
            return rsh(w, F32)
        seq = tab_blocks * tm
        return (pl.BlockSpec((None, w, tm), lambda i: (i // tab_blocks, 0, i % tab_blocks)),
                jax.ShapeDtypeStruct((t // seq, w, seq), F32))

    outs = dict(
        q=rsh(ATTN_W, BF16), qt=csh(ATTN_W, BF16), k=ret(KV_W), v=ret(KV_W),
        kh=(pl.BlockSpec((N_KV_HEADS, tm, 2 * HEAD_DIM), lambda i: (0, i, 0)),
            jax.ShapeDtypeStruct((N_KV_HEADS, t, 2 * HEAD_DIM), BF16)),
        vt=(pl.BlockSpec((N_KV_HEADS, HEAD_DIM + ONES_ROWS, tm), lambda i: (0, 0, i)),
            jax.ShapeDtypeStruct((N_KV_HEADS, HEAD_DIM + ONES_ROWS, t), BF16)),
        qi=rsh(IDXQ_W, BF16), qit=csh(IDXQ_W, BF16), ki=ret(IDX_DIM),
        kib=rsh(IDX_DIM, BF16), wi=rsh(LANES, F32), wit=csh(SUBLANES, F32),
        gq=rsh(GLA_K_W, F32), gk=rsh(GLA_K_W, F32), gv=rsh(GLA_V_W, F32), la=rsh(GLA_K_W, F32),
        gr=rsh(GLA_V_W, F32), ga=rsh(d_model, F32), gb=rsh(d_model, F32))
    consts = [lw["g1"], lw["w_in_t"], lw["qg"], lw["kg"], lw["kig"], lw["wa2h"], lw["wa2l"], lw["ba"]]
    res = pl.pallas_call(
        functools.partial(_proj_kernel, seq_minor=seq_minor),
        grid=(t // tm,),
        in_specs=[row(d_model)] + [const(a) for a in consts]
        + [pl.BlockSpec((tm, 3 * LANES), lambda i: (i % tab_blocks, 0)), const(lw["mavg"])],
        out_specs=[outs[n][0] for n in _PROJ_NAMES],
        out_shape=[outs[n][1] for n in _PROJ_NAMES],
        compiler_params=_cparams(("parallel",)),
        name="proj",
    )(x, *consts, tab, lw["mavg"])
    return dict(zip(_PROJ_NAMES, res))


def _col_reduce8(x, op, rows=SUBLANES):
    r, c = x.shape
    blk = 8 * rows
    if r > blk and r % blk == 0:
        parts = x.reshape(r // blk, blk, c)
        x = parts[0]
        for j in range(1, r // blk):
            x = op(x, parts[j])
        r = blk
    while r > rows:
        r //= 2
        x = op(x[:r], x[r:])
    return x


def _topk_select_t(key_ref, khi_ref, klo_ref, nc, kc, tq, n_keys, topk):
    kf = float(topk)
    tile16 = 2 * SUBLANES

    def count(pred):
        def body(c, acc):
            off = pl.multiple_of(c * kc, kc)
            kpos = off + lax.broadcasted_iota(I32, (kc, 1), 0)
            hit = pred(key_ref[pl.ds(off, kc), :], kpos)
            return acc + _col_reduce8(jnp.where(hit, 1.0, 0.0), jnp.add)
        acc = lax.fori_loop(0, nc, body, jnp.zeros((SUBLANES, tq), F32))
        return jnp.sum(acc, axis=0, keepdims=True)

    def count16(ref, pred):
        def body(c, acc):
            off = pl.multiple_of(c * kc, kc)
            hit = pred(ref[pl.ds(off, kc), :])
            return acc + _col_reduce8(jnp.where(hit, jnp.int16(1), jnp.int16(0)), jnp.add, tile16)
        acc = lax.fori_loop(0, nc, body, jnp.zeros((tile16, tq), I16))
        return jnp.sum(acc.astype(I32), axis=0, keepdims=True).astype(F32)

    def bisect16(ref, target):
        cur = jnp.where(count16(ref, lambda x: x >= jnp.int16(0)) >= target, 0, INT16_MIN).astype(I32)

        def bit_body(j, cur):
            cand = cur | jnp.left_shift(jnp.int32(1), 14 - j)
            cand16 = cand.astype(I16)
            return jnp.where(count16(ref, lambda x: x >= cand16) >= target, cand, cur)

        return lax.fori_loop(0, 15, bit_body, cur)

    hi = bisect16(khi_ref, kf)
    hi16 = hi.astype(I16)
    above = count16(khi_ref, lambda x: x > hi16)

    def keep_equal_hi(c, carry):
        rows = pl.ds(pl.multiple_of(c * kc, kc), kc)
        klo_ref[rows, :] = jnp.where(khi_ref[rows, :] == hi16, klo_ref[rows, :], jnp.int16(INT16_MIN))
        return carry

    lax.fori_loop(0, nc, keep_equal_hi, 0)
    lo = bisect16(klo_ref, kf - above)
    thr = jnp.left_shift(hi, 16) | (lo - INT16_MIN)
    need = kf - count(lambda k, p: k > thr)
    excess = count(lambda k, p: k == thr) - need
    excess = jnp.where(thr <= KEY_NEG_INF, 0.0, excess)
    n_bits = max(1, int(np.ceil(np.log2(n_keys))))

    def find_cut():
        def body(j, v):
            cand = v | jnp.left_shift(jnp.int32(1), n_bits - 1 - j)
            c = count(lambda k, p: (k == thr) & (p < cand))
            return jnp.where(c < need, cand, v)
        return lax.fori_loop(0, n_bits, body, jnp.zeros((1, tq), I32))

    cut = lax.cond(jnp.max(excess) > 0.0, find_cut, lambda: jnp.full((1, tq), n_keys, I32))

    def mask_of(keys, kpos):
        return (keys > thr) | ((keys == thr) & (kpos <= cut))
    return mask_of


def _attn_prompt_kernel(qt_ref, qit_ref, wit_ref, kh_ref, vt_ref, kib_ref, o_ref,
                        key_ref, khi_ref, klo_ref, bias_ref, ot_ref, qa_ref, acc_ref,
                        *, topk, kc):
    tq = qt_ref.shape[1]
    n_keys = kib_ref.shape[0]
    i = pl.program_id(1)
    nc = ((i + 1) * tq + kc - 1) // kc
    qpos = i * tq + lax.broadcasted_iota(I32, (1, tq), 1)
    wit = wit_ref[...]

    def score_chunk(c, carry):
        off = pl.multiple_of(c * kc, kc)
        kic = kib_ref[pl.ds(off, kc), :]
        acc = jnp.zeros((kc, tq), F32)
        for h in range(IDX_HEADS):
            s = _dot(kic, qit_ref[h * IDX_DIM:(h + 1) * IDX_DIM, :])
            acc = acc + wit[h:h + 1, :] * jnp.maximum(s, 0.0)
        kpos = off + lax.broadcasted_iota(I32, (kc, 1), 0)
        key = _float_key(jnp.where(kpos <= qpos, acc, -jnp.inf))
        key_ref[pl.ds(off, kc), :] = key
        khi_ref[pl.ds(off, kc), :] = (key >> 16).astype(I16)
        klo_ref[pl.ds(off, kc), :] = ((key & 0xFFFF) + INT16_MIN).astype(I16)
        return carry

    lax.fori_loop(0, nc, score_chunk, 0)
    mask_of = _topk_select_t(key_ref, khi_ref, klo_ref, nc, kc, tq, n_keys, topk)

    def bias_chunk(c, carry):
        off = pl.multiple_of(c * kc, kc)
        kpos = off + lax.broadcasted_iota(I32, (kc, 1), 0)
        sel = mask_of(key_ref[pl.ds(off, kc), :], kpos) & (kpos <= qpos)
        bias_ref[pl.ds(off, kc), :] = jnp.where(sel, 0.0, NEG_BIG)
        return carry

    lax.fori_loop(0, nc, bias_chunk, 0)

    row = lax.broadcasted_iota(I32, (2 * HEAD_DIM, 1), 0)

    def q_operand(g, m):
        heads = range(g * KV_REP, (g + 1) * KV_REP)
        qcat = jnp.concatenate([qt_ref[h * HEAD_DIM:(h + 1) * HEAD_DIM, :] for h in heads], axis=1).astype(F32)
        qa = jnp.concatenate([qcat, jnp.zeros_like(qcat)], axis=0)
        if m is not None:
            m_hi = m.astype(BF16).astype(F32)
            qa = jnp.where(row == HEAD_DIM, -m_hi, jnp.where(row == HEAD_DIM + 1, m_hi - m, qa))
        return qa.astype(BF16)

    def chunk_bias(c):
        off = pl.multiple_of(c * kc, kc)
        b = bias_ref[pl.ds(off, kc), :]
        return off, jnp.concatenate([b] * KV_REP, axis=1)

    def logits(g, off, b2):
        return _dot(kh_ref[g, pl.ds(off, kc), :], qa_ref[g]) + b2

    for g in range(N_KV_HEADS):
        qa_ref[g] = q_operand(g, None)

    def pass_max(c, mparts):
        off, b2 = chunk_bias(c)
        return tuple(jnp.maximum(mp, _col_reduce8(logits(g, off, b2), jnp.maximum)) for g, mp in enumerate(mparts))

    mparts = lax.fori_loop(0, nc, pass_max,
                           tuple(jnp.full((SUBLANES, KV_REP * tq), NEG_BIG, F32) for _ in range(N_KV_HEADS)))
    for g in range(N_KV_HEADS):
        qa_ref[g] = q_operand(g, jnp.max(mparts[g], axis=0, keepdims=True))
    acc_ref[...] = jnp.zeros_like(acc_ref)

    def pass_acc(c, carry):
        off, b2 = chunk_bias(c)
        for g in range(N_KV_HEADS):
            pb = jnp.exp2(logits(g, off, b2)).astype(BF16)
            vt = vt_ref[g, :, pl.ds(off, kc)]
            for r in range(KV_REP):
                acc_ref[g * KV_REP + r] += _dot(vt, pb[:, r * tq:(r + 1) * tq])
        return carry

    lax.fori_loop(0, nc, pass_acc, 0)
    for h in range(N_HEADS):
        ot_ref[h * HEAD_DIM:(h + 1) * HEAD_DIM, :] = acc_ref[h, :HEAD_DIM, :] / acc_ref[h, HEAD_DIM:HEAD_DIM + 1, :]

    o_ref[...] = ot_ref[...].T.astype(o_ref.dtype)


def _attn_prompt(p, n_seq, seq, topk, tq, kc):
    t = p["qt"].shape[1]
    nq = seq // tq
    qcol = lambda w: pl.BlockSpec((w, tq), lambda b, i: (0, b * nq + i))
    return pl.pallas_call(
        functools.partial(_attn_prompt_kernel, topk=topk, kc=kc),
        grid=(n_seq, nq),
        in_specs=[qcol(ATTN_W), qcol(IDXQ_W), qcol(SUBLANES),
                  pl.BlockSpec((N_KV_HEADS, seq, 2 * HEAD_DIM), lambda b, i: (0, b, 0)),
                  pl.BlockSpec((N_KV_HEADS, HEAD_DIM + ONES_ROWS, seq), lambda b, i: (0, 0, b)),
                  pl.BlockSpec((seq, IDX_DIM), lambda b, i: (b, 0))],
        out_specs=pl.BlockSpec((tq, ATTN_W), lambda b, i: (b * nq + i, 0)),
        out_shape=jax.ShapeDtypeStruct((t, ATTN_W), BF16),
        scratch_shapes=[pltpu.VMEM((seq, tq), I32), pltpu.VMEM((seq, tq), I16), pltpu.VMEM((seq, tq), I16),
                        pltpu.VMEM((seq, tq), F32), pltpu.VMEM((ATTN_W, tq), F32),
                        pltpu.VMEM((N_KV_HEADS, 2 * HEAD_DIM, KV_REP * tq), BF16),
                        pltpu.VMEM((N_HEADS, HEAD_DIM + ONES_ROWS, tq), F32)],
        compiler_params=_cparams(("parallel", "parallel")),
        name="attn_prompt",
    )(p["qt"], p["qit"], p["wit"], p["kh"], p["vt"], p["kib"])


def _count_rows(mask):
    return jnp.sum(jnp.where(mask, 1.0, 0.0), axis=1, keepdims=True)


def _topk_select(key_ref, rows, n_keys, topk):
    kf = float(topk)
    cur = jnp.where(_count_rows(key_ref[...] >= 0) >= kf, 0, INT_MIN).astype(I32)

    def bit_body(j, cur):
        cand = cur | jnp.left_shift(jnp.int32(1), 30 - j)
        return jnp.where(_count_rows(key_ref[...] >= cand) >= kf, cand, cur)

    thr = lax.fori_loop(0, 31, bit_body, cur)
    need = kf - _count_rows(key_ref[...] > thr)
    excess = _count_rows(key_ref[...] == thr) - need
    excess = jnp.where(thr <= KEY_NEG_INF, 0.0, excess)
    n_bits = max(1, int(np.ceil(np.log2(n_keys))))

    def find_cut():
        def body(j, v):
            cand = v | jnp.left_shift(jnp.int32(1), n_bits - 1 - j)
            pos = lax.broadcasted_iota(I32, (rows, n_keys), 1)
            c = _count_rows((key_ref[...] == thr) & (pos < cand))
            return jnp.where(c < need, cand, v)
        return lax.fori_loop(0, n_bits, body, jnp.zeros((rows, 1), I32))

    cut = lax.cond(jnp.max(excess) > 0.0, find_cut, lambda: jnp.full((rows, 1), n_keys, I32))

    def mask_of(keys, pos):
        return (keys > thr) | ((keys == thr) & (pos <= cut))
    return mask_of


def _attn_sample_kernel(pt_ref, *refs, topk, past_len, group, per_step):
    n_in = group * per_step
    ck, cv, cki = refs[:n_in], refs[n_in:2 * n_in], refs[2 * n_in:3 * n_in]
    (kn_ref, vn_ref, kin_ref, qi_ref, wi_ref, q_ref, o_ref,
     ks_ref, vs_ref, kis_ref, key_ref, bias_ref) = refs[3 * n_in:]
    p = pl.program_id(1)
    page = ck[0].shape[1]
    t_new = kn_ref.shape[1]
    n_keys = past_len + LANES
    rows = group * t_new
    for j in range(group):
        for u in range(per_step):
            off = pl.multiple_of((p * per_step + u) * page, page)
            ks_ref[j, :, pl.ds(off, page)] = ck[j * per_step + u][...].astype(BF16)
            vs_ref[j, :, pl.ds(off, page)] = cv[j * per_step + u][...].astype(BF16)
            kis_ref[j, :, pl.ds(off, page)] = cki[j * per_step + u][...].astype(BF16)

    @pl.when(p == pl.num_programs(1) - 1)
    def _():
        def padded(new):
            pad = jnp.zeros((LANES - t_new, new.shape[1]), F32)
            return jnp.concatenate([new, pad], axis=0).astype(BF16)

        kpos = lax.broadcasted_iota(I32, (1, n_keys), 1)
        qpos1 = past_len + lax.broadcasted_iota(I32, (t_new, 1), 0)
        for j in range(group):
            qi = qi_ref[j]
            s = jnp.concatenate([_dot(qi, kis_ref[j]), _dot_nt(qi, padded(kin_ref[j]))], axis=1)
            s = wi_ref[j] * jnp.maximum(s, 0.0)
            score = s[0:t_new]
            for h in range(1, IDX_HEADS):
                score = score + s[h * t_new:(h + 1) * t_new]
            key_ref[j * t_new:(j + 1) * t_new, :] = _float_key(jnp.where(kpos <= qpos1, score, -jnp.inf))

        mask_of = _topk_select(key_ref, rows, n_keys, topk)
        qpos = past_len + lax.rem(lax.broadcasted_iota(I32, (rows, 1), 0), t_new)
        bias_ref[...] = jnp.where(mask_of(key_ref[...], kpos) & (kpos <= qpos), 0.0, NEG_BIG)

        for j in range(group):
            b1 = bias_ref[j * t_new:(j + 1) * t_new, :]
            bias = jnp.concatenate([b1] * KV_REP, axis=0)
            kn = padded(kn_ref[j])
            vn = padded(vn_ref[j])
            for g in range(N_KV_HEADS):
                gs = slice(g * HEAD_DIM, (g + 1) * HEAD_DIM)
                qg = q_ref[j, g]
                s = jnp.concatenate([_dot(qg, ks_ref[j, gs, :]), _dot_nt(qg, kn[:, gs])], axis=1) + bias
                m = jnp.max(s, axis=1, keepdims=True)
                pr = jnp.exp2(s - m)
                l = jnp.sum(pr, axis=1, keepdims=True)
                pb = pr.astype(BF16)
                o = (_dot_nt(pb[:, :past_len], vs_ref[j, gs, :]) + _dot(pb[:, past_len:], vn[:, gs])) / l
                for r in range(KV_REP):
                    h = g * KV_REP + r
                    o_ref[j, :, h * HEAD_DIM:(h + 1) * HEAD_DIM] = o[r * t_new:(r + 1) * t_new].astype(o_ref.dtype)


def _attn_sample(page_table, ckt, cvt, ckit, layer, k_new, v_new, ki_new, qi_r, wi_r, q_r, topk, group):
    n_seq, n_pages = page_table.shape
    page = ckt.shape[3]
    past_len = n_pages * page
    t_new = k_new.shape[1]
    n_keys = past_len + LANES
    rows = group * t_new
    pt = page_table.reshape(-1)
    per_step = _tile_rows(n_pages, PAGES_PER_STEP)
    n_in = group * per_step

    def cspecs(w):
        def spec(j, u):
            return pl.BlockSpec((None, None, w, page),
                                lambda b, p, pt: (layer, pt[(b * group + j) * n_pages + p * per_step + u], 0, 0))
        return [spec(j, u) for j in range(group) for u in range(per_step)]

    gspec = lambda *s: pl.BlockSpec((group,) + s, lambda b, p, pt: (b,) + (0,) * len(s))
    return pl.pallas_call(
        functools.partial(_attn_sample_kernel, topk=topk, past_len=past_len, group=group, per_step=per_step),
        grid_spec=pltpu.PrefetchScalarGridSpec(
            num_scalar_prefetch=1,
            grid=(n_seq // group, n_pages // per_step),
            in_specs=cspecs(KV_W) + cspecs(KV_W) + cspecs(IDX_DIM)
            + [gspec(t_new, KV_W), gspec(t_new, KV_W), gspec(t_new, IDX_DIM),
               gspec(IDX_HEADS * t_new, IDX_DIM), gspec(IDX_HEADS * t_new, 1),
               gspec(N_KV_HEADS, KV_REP * t_new, HEAD_DIM)],
            out_specs=gspec(t_new, ATTN_W),
            scratch_shapes=[pltpu.VMEM((group, KV_W, past_len), BF16), pltpu.VMEM((group, KV_W, past_len), BF16),
                            pltpu.VMEM((group, IDX_DIM, past_len), BF16),
                            pltpu.VMEM((rows, n_keys), I32), pltpu.VMEM((rows, n_keys), F32)]),
        out_shape=jax.ShapeDtypeStruct((n_seq, t_new, ATTN_W), BF16),
        compiler_params=_cparams(("parallel", "arbitrary")),
        name="attn_sample",
    )(pt, *([ckt] * n_in), *([cvt] * n_in), *([ckit] * n_in), k_new, v_new, ki_new, qi_r, wi_r, q_r)


def _gla_span(q_ref, k_ref, v_ref, la_ref, o_ref, st_ref, s, r0, size):
    rows = pl.ds(r0, size)
    r = lax.broadcasted_iota(I32, (size, size), 0)
    c = lax.broadcasted_iota(I32, (size, size), 1)
    causal = r >= c
    tril = jnp.where(causal, 1.0, 0.0).astype(BF16)
    eye = (lax.broadcasted_iota(I32, (GLA_DK, GLA_DK), 0) == lax.broadcasted_iota(I32, (GLA_DK, GLA_DK), 1))
    b = sum(_dot(tril, part) for part in _split3(la_ref[s, rows, :]))
    b_end = b[size - 1:size, :]
    b_mid = b[size // 2 - 1:size // 2, :]
    q = q_ref[s, rows, :]
    k = k_ref[s, rows, :]
    v = v_ref[s, rows, :].astype(BF16)
    q_in = (q * jnp.exp(b)).astype(BF16)
    q_d = (q * jnp.exp(b - b_mid)).astype(BF16)
    k_d = (k * jnp.exp(b_mid - b)).astype(BF16)
    k_out = (k * jnp.exp(b_end - b)).astype(BF16)
    for h in range(GLA_HEADS):
        ks = slice(h * GLA_DK, (h + 1) * GLA_DK)
        vs = slice(h * GLA_DV, (h + 1) * GLA_DV)
        dec = jnp.exp(jnp.sum(jnp.where(eye, b_end[:, ks], 0.0), axis=1, keepdims=True))
        st = st_ref[s, h]
        a = jnp.where(causal, _dot_nt(q_d[:, ks], k_d[:, ks]), 0.0).astype(BF16)
        o_ref[s, rows, vs] = _dot(q_in[:, ks], st.astype(BF16)) + _dot(a, v[:, vs])
        st_ref[s, h] = st * dec + _dot_tn(k_out[:, ks], v[:, vs])


def _gla_kernel(q_ref, k_ref, v_ref, la_ref, s0_ref, o_ref, sf_ref, st_ref, *, chunk, sub):
    ns, tb = q_ref.shape[0], q_ref.shape[1]
    span = functools.partial(_gla_span, q_ref, k_ref, v_ref, la_ref, o_ref, st_ref)

    @pl.when(pl.program_id(1) == 0)
    def _():
        st_ref[...] = s0_ref[...]

    def body(i, carry):
        r0 = pl.multiple_of(i * chunk, chunk)
        if chunk == sub:
            for s in range(ns):
                span(s, r0, chunk)
            return carry
        tot = [jnp.sum(la_ref[s, pl.ds(r0, chunk), :], axis=0, keepdims=True) for s in range(ns)]
        mild = jnp.min(jnp.concatenate(tot, axis=0)) > -GLA_MAX_CHUNK_LOG_DECAY

        @pl.when(mild)
        def _():
            for s in range(ns):
                span(s, r0, chunk)

        @pl.when(jnp.logical_not(mild))
        def _():
            for s in range(ns):
                for j in range(chunk // sub):
                    span(s, r0 + j * sub, sub)
        return carry

    lax.fori_loop(0, tb // chunk, body, 0)
    sf_ref[...] = st_ref[...]


def _gla(gq, gk, gv, la, s0, n_seq, seq, ns, tb, chunk, sub):
    nt = seq // tb
    r3 = lambda a: a.reshape(n_seq, seq, a.shape[-1])
    row = lambda w: pl.BlockSpec((ns, tb, w), lambda b, i: (b, i, 0))
    sspec = pl.BlockSpec((ns, GLA_HEADS, GLA_DK, GLA_DV), lambda b, i: (b, 0, 0, 0))
    o, s_fin = pl.pallas_call(
        functools.partial(_gla_kernel, chunk=chunk, sub=sub),
        grid=(n_seq // ns, nt),
        in_specs=[row(GLA_K_W), row(GLA_K_W), row(GLA_V_W), row(GLA_K_W), sspec],
        out_specs=[row(GLA_V_W), sspec],
        out_shape=[jax.ShapeDtypeStruct((n_seq, seq, GLA_V_W), F32),
                   jax.ShapeDtypeStruct((n_seq, GLA_HEADS, GLA_DK, GLA_DV), F32)],
        scratch_shapes=[pltpu.VMEM((ns, GLA_HEADS, GLA_DK, GLA_DV), F32)],
        compiler_params=_cparams(("parallel", "arbitrary")),
        name="gla",
    )(r3(gq), r3(gk), r3(gv), r3(la), s0)
    return o.reshape(n_seq * seq, GLA_V_W), s_fin


def _merge_kernel(x_ref, at_ref, gl_ref, gr_ref, ga_ref, gb_ref, gon_ref, woa_ref, wob_ref, wo_ref, n2_ref,
                  x1_o, hn_o):
    gl = gl_ref[...]
    gr = gr_ref[...]
    parts = []
    for h in range(GLA_HEADS):
        gh = gl[:, h * GLA_DV:(h + 1) * GLA_DV]
        ms = jnp.mean(gh * gh, axis=-1, keepdims=True)
        parts.append((gh * lax.rsqrt(ms + EPS)) * gon_ref[...])
    g_out = jnp.concatenate(parts, axis=1) * (gr * _sigmoid(gr))
    merged = (_sigmoid(ga_ref[...]) * _dot(at_ref[...], woa_ref[...])
              + _sigmoid(gb_ref[...]) * _dot(g_out.astype(BF16), wob_ref[...]))
    x1 = x_ref[...] + _dot(merged.astype(BF16), wo_ref[...])
    x1_o[...] = x1
    ms = jnp.mean(x1 * x1, axis=-1, keepdims=True)
    hn_o[...] = ((x1 * lax.rsqrt(ms + EPS)) * n2_ref[...]).astype(BF16)


def _merge(x, attn_o, gla_o, gr, gate_a, gate_b, lw, tm):
    t, d_model = x.shape
    row = lambda w: pl.BlockSpec((tm, w), lambda i: (i, 0))
    const = lambda a: pl.BlockSpec(a.shape, lambda i: (0,) * a.ndim)
    consts = [lw["gon"], lw["w_oa"], lw["w_ob"], lw["w_o"], lw["n2"]]
    return pl.pallas_call(
        _merge_kernel,
        grid=(t // tm,),
        in_specs=[row(d_model), row(ATTN_W), row(GLA_V_W), row(GLA_V_W), row(d_model), row(d_model)]
        + [const(a) for a in consts],
        out_specs=[row(d_model), row(d_model)],
        out_shape=[jax.ShapeDtypeStruct((t, d_model), F32), jax.ShapeDtypeStruct((t, d_model), BF16)],
        compiler_params=_cparams(("parallel",)),
        name="merge",
    )(x, attn_o, gla_o, gr, gate_a, gate_b, *consts)


def _ffn_kernel(hn_ref, x1_ref, wg_ref, wu_ref, wd_ref, y_ref, acc_ref):
    f = pl.program_id(1)

    @pl.when(f == 0)
    def _():
        acc_ref[...] = jnp.zeros_like(acc_ref)

    hn = hn_ref[...]
    gt = _dot(hn, wg_ref[...])
    up = _dot(hn, wu_ref[...])
    acc_ref[...] += _dot(((gt * _sigmoid(gt)) * up).astype(BF16), wd_ref[...])

    @pl.when(f == pl.num_programs(1) - 1)
    def _():
        y_ref[...] = x1_ref[...] + acc_ref[...]


def _ffn(hn, x1, lw, tm, tf):
    t, d_model = x1.shape
    d_ff = lw["w_g"].shape[1]
    row = pl.BlockSpec((tm, d_model), lambda i, f: (i, 0))
    return pl.pallas_call(
        _ffn_kernel,
        grid=(t // tm, d_ff // tf),
        in_specs=[row, row,
                  pl.BlockSpec((d_model, tf), lambda i, f: (0, f)),
                  pl.BlockSpec((d_model, tf), lambda i, f: (0, f)),
                  pl.BlockSpec((tf, d_model), lambda i, f: (f, 0))],
        out_specs=row,
        out_shape=jax.ShapeDtypeStruct((t, d_model), F32),
        scratch_shapes=[pltpu.VMEM((tm, d_model), F32)],
        compiler_params=_cparams(("parallel", "arbitrary")),
        name="ffn",
    )(hn, x1, lw["w_g"], lw["w_u"], lw["w_d"])


def _pack_layer(l, norm1_g, w_in, q_norm_g, k_norm_g, kidx_norm_g, gla_wa2, gla_ba, gla_onorm_g,
                w_oa, w_ob, w_o, norm2_g, w_ffn_gu, w_ffn_down):
    d_model = w_in.shape[1]
    wt = jnp.transpose(w_in, (2, 0, 1))[:, l, :]
    splits = (ATTN_W, KV_W, KV_W, IDXQ_W, IDX_DIM, IDX_HEADS, GLA_K_W, GLA_K_W, GLA_V_W, GLA_GATE_RANK,
              GLA_V_W, d_model, d_model)
    offs = np.cumsum(splits)[:-1].tolist()
    q, k, v, qi, ki, wi, gq, gk, gv, ga, gr, gate_a, gate_b = jnp.split(wt, offs, axis=0)
    z = lambda n: jnp.zeros((n, d_model), wt.dtype)
    packed = jnp.concatenate(
        [q, k, v, qi, ki, wi, z(LANES - IDX_DIM - IDX_HEADS), gq, gk, gv, ga, z(LANES - GLA_GATE_RANK),
         gr, gate_a, gate_b], axis=0).astype(BF16)
    wa2 = jnp.concatenate([gla_wa2[l], jnp.zeros((LANES - GLA_GATE_RANK, GLA_K_W), F32)], axis=0)
    wa2h, wa2l = _split_hi_lo(wa2)
    blk = np.arange(ATTN_W) // HEAD_DIM
    mavg = jnp.asarray((blk[:, None] == blk[None, :]).astype(np.float32) / HEAD_DIM, BF16)
    d_ff = w_ffn_down.shape[1]
    return dict(
        g1=norm1_g[l][None, :], w_in_t=packed,
        qg=jnp.tile(q_norm_g[l], N_HEADS)[None, :], kg=jnp.tile(k_norm_g[l], N_KV_HEADS)[None, :],
        kig=jnp.concatenate([kidx_norm_g[l], jnp.ones((LANES - IDX_DIM,), F32)])[None, :],
        wa2h=wa2h, wa2l=wa2l, ba=gla_ba[l][None, :], mavg=mavg,
        gon=gla_onorm_g[l][None, :], w_oa=w_oa[l].astype(BF16), w_ob=w_ob[l].astype(BF16),
        w_o=w_o[l].astype(BF16), n2=norm2_g[l][None, :],
        w_g=w_ffn_gu[l][:, :d_ff].astype(BF16), w_u=w_ffn_gu[l][:, d_ff:].astype(BF16),
        w_d=w_ffn_down[l].astype(BF16))


def _rope_table(pos):
    half = ROT_DIM // 2
    inv = jnp.power(ROPE_THETA, -jnp.arange(half, dtype=F32) * 2.0 / ROT_DIM)
    ang = pos.astype(F32)[:, None] * inv[None, :]
    cos, sin = jnp.cos(ang), jnp.sin(ang)
    n = pos.shape[0]
    rest = HEAD_DIM - ROT_DIM
    one = jnp.concatenate([cos, cos, jnp.ones((n, rest), F32)], axis=1)
    sa = jnp.concatenate([-sin, jnp.zeros((n, half + rest), F32)], axis=1)
    sb = jnp.concatenate([jnp.zeros((n, half), F32), sin, jnp.zeros((n, rest), F32)], axis=1)
    rep = LANES // HEAD_DIM
    return jnp.concatenate([jnp.tile(one, (1, rep)), jnp.tile(sa, (1, rep)), jnp.tile(sb, (1, rep))], axis=1)


def _tile_rows(n, want):
    t = min(n, want)
    while n % t:
        t //= 2
    return t


def _tail(x, attn_o, p, lw):
    t = x.shape[0]
    x1, hn = _merge(x, attn_o, p["gla_o"], p["gr"], p["ga"], p["gb"], lw, _tile_rows(t, 512))
    d_ff = lw["w_g"].shape[1]
    tf = d_ff // 2 if (d_ff // 2) % LANES == 0 else d_ff
    return _ffn(hn, x1, lw, _tile_rows(t, 512), tf)


def kernel(x_prompt, x_sample, cache_k, cache_v, cache_kidx, state_gla, page_table, norm1_g, w_in, q_norm_g,
           k_norm_g, kidx_norm_g, gla_wa2, gla_ba, gla_onorm_g, w_oa, w_ob, w_o, norm2_g, w_ffn_gu, w_ffn_down):
    n_b, seq, d_model = x_prompt.shape
    n_bd, t_d, _ = x_sample.shape
    depth = w_in.shape[0]
    n_pool, page = cache_k.shape[1], cache_k.shape[2]
    past_len = page_table.shape[1] * page
    topk_p = min(TOPK_MAX, seq // 4)
    topk_s = min(TOPK_MAX, (past_len + t_d) // 4)

    tab_p = _rope_table(jnp.arange(seq, dtype=I32))
    tab_s = jnp.tile(_rope_table(past_len + jnp.arange(t_d, dtype=I32)), (n_bd, 1))
    ckt = jnp.transpose(cache_k, (0, 1, 3, 4, 2)).reshape(depth, n_pool, KV_W, page)
    cvt = jnp.transpose(cache_v, (0, 1, 3, 4, 2)).reshape(depth, n_pool, KV_W, page)
    ckit = jnp.transpose(cache_kidx, (0, 1, 3, 2))

    hp = x_prompt.reshape(n_b * seq, d_model)
    hs = x_sample.reshape(n_bd * t_d, d_model)
    tm_p = _tile_rows(seq, 512)
    tm_s = _tile_rows(n_bd * t_d, 256)
    tq = _tile_rows(seq, 256)
    kc = _tile_rows(seq, 512)
    tb = _tile_rows(seq, 512)
    chunk_p = _tile_rows(tb, GLA_CHUNK)
    sub_p = _tile_rows(chunk_p, GLA_SUB)
    group = _tile_rows(n_bd, SAMPLE_GROUP)
    outs = [[] for _ in range(8)]
    for l in range(depth):
        lw = _pack_layer(l, norm1_g, w_in, q_norm_g, k_norm_g, kidx_norm_g, gla_wa2, gla_ba, gla_onorm_g,
                         w_oa, w_ob, w_o, norm2_g, w_ffn_gu, w_ffn_down)

        p = _proj(hp, lw, tab_p, seq // tm_p, tm_p, True)
        attn_o = _attn_prompt(p, n_b, seq, topk_p, tq, kc)
        s0 = jnp.zeros((n_b, GLA_HEADS, GLA_DK, GLA_DV), F32)
        p["gla_o"], s_fin = _gla(p["gq"], p["gk"], p["gv"], p["la"], s0, n_b, seq, _tile_rows(n_b, GLA_SEQS), tb,
                                 chunk_p, sub_p)
        hp = _tail(hp, attn_o, p, lw)
        outs[0].append(p["k"].reshape(n_b, N_KV_HEADS, HEAD_DIM, seq).transpose(0, 3, 1, 2))
        outs[1].append(p["v"].reshape(n_b, N_KV_HEADS, HEAD_DIM, seq).transpose(0, 3, 1, 2))
        outs[2].append(p["ki"].transpose(0, 2, 1))
        outs[3].append(s_fin)

        p = _proj(hs, lw, tab_s, (n_bd * t_d) // tm_s, tm_s, False)
        qi_r = p["qi"].reshape(n_bd, t_d, IDX_HEADS, IDX_DIM).transpose(0, 2, 1, 3)
        qi_r = qi_r.reshape(n_bd, IDX_HEADS * t_d, IDX_DIM)
        wi_r = p["wi"][:, IDX_DIM:IDX_DIM + IDX_HEADS].reshape(n_bd, t_d, IDX_HEADS).transpose(0, 2, 1)
        wi_r = wi_r.reshape(n_bd, IDX_HEADS * t_d, 1)
        q_r = p["q"].reshape(n_bd, t_d, N_KV_HEADS, KV_REP, HEAD_DIM).transpose(0, 2, 3, 1, 4)
        q_r = q_r.reshape(n_bd, N_KV_HEADS, KV_REP * t_d, HEAD_DIM)
        attn_o = _attn_sample(page_table, ckt, cvt, ckit, l, p["k"].reshape(n_bd, t_d, KV_W),
                              p["v"].reshape(n_bd, t_d, KV_W), p["ki"].reshape(n_bd, t_d, IDX_DIM),
                              qi_r, wi_r, q_r, topk_s, group)
        attn_o = attn_o.reshape(n_bd * t_d, ATTN_W)
        p["gla_o"], s_new = _gla(p["gq"], p["gk"], p["gv"], p["la"], state_gla[l], n_bd, t_d,
                                 _tile_rows(n_bd, GLA_SEQS), t_d, t_d, t_d)
        hs = _tail(hs, attn_o, p, lw)
        outs[4].append(p["k"].reshape(n_bd, t_d, N_KV_HEADS, HEAD_DIM))
        outs[5].append(p["v"].reshape(n_bd, t_d, N_KV_HEADS, HEAD_DIM))
        outs[6].append(p["ki"].reshape(n_bd, t_d, IDX_DIM))
        outs[7].append(s_new)

    y_prompt = hp.reshape(n_b, seq, d_model)
    y_sample = hs.reshape(n_bd, t_d, d_model)
    return (y_prompt, y_sample) + tuple(jnp.stack(o) for o in outs)
```
